```python
import jax
import jax.numpy as jnp
from jax import lax
import numpy as np


D_MODEL = 1024
BATCH = 8
SEQ = 2048
DEPTH = 2

MEM_LEN = 256
N_BRANCH = 4
BRANCH_W = 512
BLOCK_Q = 128
A_GROUPS = 4
A_CHUNK = 128
A_GW = BRANCH_W // A_GROUPS
B_HEADS = 8
B_KV = 2
B_GQ = B_HEADS // B_KV
B_HD = 64
CMP_LEN = 32
CMP_STRIDE = 16
SEL_LEN = 64
SEL_N = 8
WIN = 256
C_HEADS = 8
C_HD = 64
M_HEADS = 4
M_HD = BRANCH_W // M_HEADS
ROPE_THETA = 10000.0
EPS = 1e-6
NEG = -1e30
BIG = 1e9

IN_SIZES = (
    BRANCH_W, BRANCH_W, BRANCH_W,
    B_HEADS * B_HD, B_KV * B_HD, B_KV * B_HD, B_KV * B_HD,
    B_KV * B_HD, B_KV * B_HD, B_KV * B_HD, B_HEADS * 3, BRANCH_W,
    C_HEADS * C_HD, C_HEADS * C_HD, C_HEADS * C_HD, C_HEADS, BRANCH_W,
    M_HEADS * M_HD, BRANCH_W,
)
D_IN = int(sum(IN_SIZES))
IN_SPLITS = tuple(int(c) for c in np.cumsum(IN_SIZES)[:-1])

kernel_name = 'hybrid_gmlp_nsa_fox_mem_block'


def rms_norm(x, g):
    xf = x.astype(jnp.float32)
    y = xf * lax.rsqrt(jnp.mean(xf * xf, axis=-1, keepdims=True) + EPS)
    return (y * g.astype(jnp.float32)).astype(x.dtype)


def layer_norm(x, g, b):
    xf = x.astype(jnp.float32)
    mu = jnp.mean(xf, axis=-1, keepdims=True)
    xc = xf - mu
    y = xc * lax.rsqrt(jnp.mean(xc * xc, axis=-1, keepdims=True) + EPS)
    return (y * g.astype(jnp.float32) + b.astype(jnp.float32)).astype(x.dtype)


def rope(x, pos):
    half = x.shape[-1] // 2
    freqs = ROPE_THETA ** (-jnp.arange(half, dtype=jnp.float32) / half)
    ang = pos.astype(jnp.float32)[:, None] * freqs[None, :]
    cos, sin = jnp.cos(ang), jnp.sin(ang)
    xf = x.astype(jnp.float32)
    x1, x2 = xf[..., :half], xf[..., half:]
    return jnp.concatenate([x1 * cos - x2 * sin, x1 * sin + x2 * cos], axis=-1).astype(x.dtype)


def masked_softmax(s, mask):
    p = jax.nn.softmax(jnp.where(mask, s, NEG), axis=-1)
    return jnp.where(mask, p, 0.0)


def chunked_gmlp(u, v, ln_g, ln_b, w_s, b_s):
    Bn, S, W = u.shape
    u = jax.nn.gelu(u)
    v = layer_norm(jax.nn.gelu(v), ln_g, ln_b)
    vc = v.reshape(Bn, S // A_CHUNK, A_CHUNK, A_GROUPS, A_GW)
    tri = np.tril(np.ones((A_CHUNK, A_CHUNK), dtype=bool))
    ws = jnp.where(tri, w_s, 0.0)
    s = jnp.einsum('gts,bcsgd->bctgd', ws, vc) + b_s.T[None, None, :, :, None]
    return u * s.reshape(Bn, S, W)


def compress(k, pos_emb, w1, w2):
    S = k.shape[2]
    nc = (S - CMP_LEN) // CMP_STRIDE + 1
    idx = np.arange(nc)[:, None] * CMP_STRIDE + np.arange(CMP_LEN)[None, :]
    blocks = k[:, :, idx] + pos_emb
    flat = blocks.reshape(blocks.shape[0], blocks.shape[1], nc, CMP_LEN * B_HD)
    return jax.nn.silu(flat @ w1) @ w2


def nsa_attention(q, kc, vc, ks, vs, kw, vw, gate_logits, pos_k, w1_k, w2_k, pos_v, w1_v, w2_v):
    Bn, S, _ = q.shape
    pos = jnp.arange(S)
    scale = B_HD ** -0.5
    q = rope(q.reshape(Bn, S, B_KV, B_GQ, B_HD).transpose(0, 2, 3, 1, 4), pos)
    kv_heads = lambda t: t.reshape(Bn, S, B_KV, B_HD).transpose(0, 2, 1, 3)
    kc, ks, kw = [rope(kv_heads(t), pos) for t in (kc, ks, kw)]
    vc, vs, vw = [kv_heads(t) for t in (vc, vs, vw)]

    kcmp = compress(kc, pos_k, w1_k, w2_k)
    vcmp = compress(vc, pos_v, w1_v, w2_v)
    nc = kcmp.shape[2]
    cmp_ok = (np.arange(nc) * CMP_STRIDE + CMP_LEN - 1)[None, :] <= np.arange(S)[:, None]
    s = jnp.einsum('bgrtd,bgcd->bgrtc', q, kcmp, preferred_element_type=jnp.float32) * scale
    p_cmp = masked_softmax(s, cmp_ok)
    o_cmp = jnp.einsum('bgrtc,bgcd->bgrtd', p_cmp.astype(vcmp.dtype), vcmp)

    nsel = S // SEL_LEN
    n_sel = min(SEL_N, nsel)
    ci = np.arange(nc) * CMP_STRIDE
    sj = np.arange(nsel) * SEL_LEN
    overlap = ((ci[:, None] <= sj[None, :] + SEL_LEN - 1) & (ci[:, None] + CMP_LEN - 1 >= sj[None, :])).astype(np.float32)
    imp = jnp.einsum('bgrtc,cj->bgtj', p_cmp, overlap)
    cur = np.arange(S) // SEL_LEN
    blk = np.arange(nsel)
    forced = (blk[None, :] == 0) | (blk[None, :] == cur[:, None]) | (blk[None, :] == cur[:, None] - 1)
    future = blk[None, :] > cur[:, None]
    imp = jnp.where(forced, BIG, jnp.where(future, NEG, imp))
    _, sel_idx = lax.top_k(imp, n_sel)

    nb = S // BLOCK_Q
    q_blk = q.reshape(Bn, B_KV, B_GQ, nb, BLOCK_Q, B_HD).transpose(3, 0, 1, 2, 4, 5)
    idx_blk = sel_idx.reshape(Bn, B_KV, nb, BLOCK_Q, n_sel).transpose(2, 0, 1, 3, 4)
    ks_b = ks.reshape(Bn, B_KV, nsel, SEL_LEN, B_HD)
    vs_b = vs.reshape(Bn, B_KV, nsel, SEL_LEN, B_HD)
    kpad = jnp.pad(kw, ((0, 0), (0, 0), (WIN, 0), (0, 0)))
    vpad = jnp.pad(vw, ((0, 0), (0, 0), (WIN, 0), (0, 0)))
    bi = jnp.arange(Bn)[:, None, None, None]
    gi = jnp.arange(B_KV)[None, :, None, None]
    n_keys = n_sel * SEL_LEN

    def block_fn(args):
        qb, idxb, i = args
        qs = i * BLOCK_Q
        tq = qs + jnp.arange(BLOCK_Q)
        kg = ks_b[bi, gi, idxb].reshape(Bn, B_KV, BLOCK_Q, n_keys, B_HD)
        vg = vs_b[bi, gi, idxb].reshape(Bn, B_KV, BLOCK_Q, n_keys, B_HD)
        kpos = (idxb[..., None] * SEL_LEN + jnp.arange(SEL_LEN)).reshape(Bn, B_KV, BLOCK_Q, n_keys)
        sel_ok = (kpos <= tq[:, None])[:, :, None]
        s_sel = jnp.einsum('bgrtd,bgtkd->bgrtk', qb, kg, preferred_element_type=jnp.float32) * scale
        o_sel = jnp.einsum('bgrtk,bgtkd->bgrtd', masked_softmax(s_sel, sel_ok).astype(vg.dtype), vg)
        kwin = lax.dynamic_slice_in_dim(kpad, qs, WIN + BLOCK_Q, axis=2)
        vwin = lax.dynamic_slice_in_dim(vpad, qs, WIN + BLOCK_Q, axis=2)
        wpos = qs - WIN + jnp.arange(WIN + BLOCK_Q)
        win_ok = (wpos[None, :] <= tq[:, None]) & (wpos[None, :] > tq[:, None] - WIN) & (wpos[None, :] >= 0)
        s_win = jnp.einsum('bgrtd,bgkd->bgrtk', qb, kwin, preferred_element_type=jnp.float32) * scale
        o_win = jnp.einsum('bgrtk,bgkd->bgrtd', masked_softmax(s_win, win_ok).astype(vwin.dtype), vwin)
        return o_sel, o_win

    o_sel, o_win = lax.map(block_fn, (q_blk, idx_blk, jnp.arange(nb)))
    o_sel = o_sel.transpose(1, 2, 3, 0, 4, 5).reshape(Bn, B_KV, B_GQ, S, B_HD)
    o_win = o_win.transpose(1, 2, 3, 0, 4, 5).reshape(Bn, B_KV, B_GQ, S, B_HD)

    g = jax.nn.sigmoid(gate_logits.astype(jnp.float32)).reshape(Bn, S, B_KV, B_GQ, 3).transpose(0, 2, 3, 1, 4).astype(q.dtype)
    o = g[..., 0:1] * o_cmp + g[..., 1:2] * o_sel + g[..., 2:3] * o_win
    return o.transpose(0, 3, 1, 2, 4).reshape(Bn, S, B_HEADS * B_HD)


def forgetting_attention(q, k, v, f_logit, f_bias):
    Bn, S, _ = q.shape
    heads = lambda t: t.reshape(Bn, S, C_HEADS, C_HD).transpose(0, 2, 1, 3)
    q, k, v = heads(q), heads(k), heads(v)
    log_f = jax.nn.log_sigmoid(f_logit.astype(jnp.float32) + f_bias.astype(jnp.float32))
    c = jnp.cumsum(log_f, axis=1).transpose(0, 2, 1)
    scale = C_HD ** -0.5
    outs = []
    for i in range(S // BLOCK_Q):
        qs, qe = i * BLOCK_Q, (i + 1) * BLOCK_Q
        s = jnp.einsum('bhtd,bhsd->bhts', q[:, :, qs:qe], k[:, :, :qe], preferred_element_type=jnp.float32) * scale
        s = s + c[:, :, qs:qe, None] - c[:, :, None, :qe]
        causal = np.arange(qe)[None, :] <= np.arange(qs, qe)[:, None]
        p = masked_softmax(s, causal)
        outs.append(jnp.einsum('bhts,bhsd->bhtd', p.astype(v.dtype), v[:, :, :qe]))
    o = jnp.concatenate(outs, axis=2)
    return o.transpose(0, 2, 1, 3).reshape(Bn, S, C_HEADS * C_HD)


def memory_attention(q, mem, g_mem, w_mem_kv):
    Bn, S, _ = q.shape
    mkv = rms_norm(mem, g_mem) @ w_mem_kv
    mk, mv = jnp.split(mkv, 2, axis=-1)
    heads = lambda t: t.reshape(t.shape[0], t.shape[1], M_HEADS, M_HD).transpose(0, 2, 1, 3)
    s = jnp.einsum('bhtd,bhmd->bhtm', heads(q), heads(mk), preferred_element_type=jnp.float32) * (M_HD ** -0.5)
    p = jax.nn.softmax(s, axis=-1)
    o = jnp.einsum('bhtm,bhmd->bhtd', p.astype(mv.dtype), heads(mv))
    return o.transpose(0, 2, 1, 3).reshape(Bn, S, M_HEADS * M_HD)


def hybrid_layer(x, mem, w_in, g_pre, g_post, g_mem, w_mem_kv, a_ln_g, a_ln_b, a_ws, a_bs,
                 b_cmp_pos_k, b_cmp_w1_k, b_cmp_w2_k, b_cmp_pos_v, b_cmp_w1_v, b_cmp_w2_v,
                 c_fbias, w_br, w_gate, w_o):
    h = rms_norm(x, g_pre)
    parts = jnp.split(h @ w_in, IN_SPLITS, axis=-1)
    (a_u, a_v, a_z,
     b_q, b_kc, b_vc, b_ks, b_vs, b_kw, b_vw, b_g, b_z,
     c_q, c_k, c_v, c_f, c_z,
     m_q, m_z) = parts
    y_a = chunked_gmlp(a_u, a_v, a_ln_g, a_ln_b, a_ws, a_bs) * jax.nn.silu(a_z)
    y_b = nsa_attention(b_q, b_kc, b_vc, b_ks, b_vs, b_kw, b_vw, b_g,
                        b_cmp_pos_k, b_cmp_w1_k, b_cmp_w2_k, b_cmp_pos_v, b_cmp_w1_v, b_cmp_w2_v) * jax.nn.silu(b_z)
    y_c = forgetting_attention(c_q, c_k, c_v, c_f, c_fbias) * jax.nn.silu(c_z)
    y_m = memory_attention(m_q, mem, g_mem, w_mem_kv) * jax.nn.silu(m_z)
    ys = jnp.stack([y_a, y_b, y_c, y_m], axis=2)
    up = jnp.einsum('bsnw,nwd->bsnd', ys, w_br)
    gates = jax.nn.sigmoid(jnp.einsum('bsd,dne->bsne', h, w_gate))
    merged = jnp.einsum('bsnd,bsnd->bsd', gates, up)
    out = merged @ w_o
    return x + rms_norm(out, g_post)


def setup_inputs(seed: int = 0) -> dict:
    key = jax.random.key(seed)
    k = jax.random.split(key, 21)
    f32 = jnp.float32
    nrm = lambda kk, shape, sc: jax.random.normal(kk, shape, f32) * sc
    L, D = DEPTH, D_MODEL
    return {
        'x': nrm(k[0], (BATCH, SEQ, D), 1.0),
        'mem': nrm(k[1], (BATCH, MEM_LEN, D), 1.0),
        'w_in': nrm(k[2], (L, D, D_IN), D ** -0.5),
        'g_pre': 1.0 + nrm(k[3], (L, D), 0.02),
        'g_post': 1.0 + nrm(k[4], (L, D), 0.02),
        'g_mem': 1.0 + nrm(k[5], (L, D), 0.02),
        'w_mem_kv': nrm(k[6], (L, D, 2 * BRANCH_W), D ** -0.5),
        'a_ln_g': 1.0 + nrm(k[7], (L, BRANCH_W), 0.02),
        'a_ln_b': nrm(k[8], (L, BRANCH_W), 0.02),
        'a_ws': nrm(k[9], (L, A_GROUPS, A_CHUNK, A_CHUNK), A_CHUNK ** -0.5),
        'a_bs': 1.0 + nrm(k[10], (L, A_GROUPS, A_CHUNK), 0.1),
        'b_cmp_pos_k': nrm(k[11], (L, CMP_LEN, B_HD), 0.02),
        'b_cmp_w1_k': nrm(k[12], (L, CMP_LEN * B_HD, B_HD), (CMP_LEN * B_HD) ** -0.5),
        'b_cmp_w2_k': nrm(k[13], (L, B_HD, B_HD), B_HD ** -0.5),
        'b_cmp_pos_v': nrm(k[14], (L, CMP_LEN, B_HD), 0.02),
        'b_cmp_w1_v': nrm(k[15], (L, CMP_LEN * B_HD, B_HD), (CMP_LEN * B_HD) ** -0.5),
        'b_cmp_w2_v': nrm(k[16], (L, B_HD, B_HD), B_HD ** -0.5),
        'c_fbias': 2.0 + nrm(k[17], (L, C_HEADS), 0.5),
        'w_br': nrm(k[18], (L, N_BRANCH, BRANCH_W, D), BRANCH_W ** -0.5),
        'w_gate': nrm(k[19], (L, D, N_BRANCH, D), D ** -0.5),
        'w_o': nrm(k[20], (L, D, D), D ** -0.5),
    }


def reference(x, mem, w_in, g_pre, g_post, g_mem, w_mem_kv, a_ln_g, a_ln_b, a_ws, a_bs,
              b_cmp_pos_k, b_cmp_w1_k, b_cmp_w2_k, b_cmp_pos_v, b_cmp_w1_v, b_cmp_w2_v,
              c_fbias, w_br, w_gate, w_o):
    for l in range(DEPTH):
        x = hybrid_layer(x, mem, w_in[l], g_pre[l], g_post[l], g_mem[l], w_mem_kv[l],
                         a_ln_g[l], a_ln_b[l], a_ws[l], a_bs[l],
                         b_cmp_pos_k[l], b_cmp_w1_k[l], b_cmp_w2_k[l],
                         b_cmp_pos_v[l], b_cmp_w1_v[l], b_cmp_w2_v[l],
                         c_fbias[l], w_br[l], w_gate[l], w_o[l])
    return x
```

```python
import functools

import numpy as np
import jax
import jax.numpy as jnp
from jax import lax
from jax.experimental import pallas as pl
from jax.experimental.pallas import tpu as pltpu

F32 = jnp.float32
BF16 = jnp.bfloat16

D_MODEL = 1024
N_BRANCH = 4
BRANCH_W = 512
A_GROUPS = 4
A_CHUNK = 128
B_HEADS = 8
B_KV = 2
B_GQ = B_HEADS // B_KV
B_HD = 64
CMP_LEN = 32
CMP_STRIDE = 16
SEL_LEN = 64
SEL_N = 8
WIN = 256
C_HEADS = 8
C_HD = 64
M_HEADS = 4
M_HD = BRANCH_W // M_HEADS
ROPE_THETA = 10000.0
EPS = 1e-6
NEG = -1e30
BIG = 1e9

IN_SIZES = (
    BRANCH_W, BRANCH_W, BRANCH_W,
    B_HEADS * B_HD, B_KV * B_HD, B_KV * B_HD, B_KV * B_HD,
    B_KV * B_HD, B_KV * B_HD, B_KV * B_HD, B_HEADS * 3, BRANCH_W,
    C_HEADS * C_HD, C_HEADS * C_HD, C_HEADS * C_HD, C_HEADS, BRANCH_W,
    M_HEADS * M_HD, BRANCH_W,
)
IN_SPLITS = tuple(int(c) for c in np.cumsum(IN_SIZES)[:-1])

LANE = 128
VMEM_LIMIT = 56 * 1024 * 1024

COL_A_U, COL_A_V, COL_A_Z = 0, 4, 8
COL_B_Q, COL_B_Z = 12, 16
COL_C_Q, COL_C_K, COL_C_V, COL_C_Z = 20, 24, 28, 32
COL_M_Q, COL_M_Z = 36, 40
COL_KS, COL_KW, COL_VS, COL_VW = 44, 46, 48, 50
COL_KC, COL_VC = 52, 53
N_COLS = 54
ROPE_COLS = frozenset(list(range(COL_B_Q, COL_B_Q + 4)) + [COL_KS, COL_KS + 1, COL_KW, COL_KW + 1, COL_KC])
W_MAIN = N_COLS * LANE
SMALL_G0, SMALL_F0 = 0, B_HEADS * 3

_NT = (((1,), (1,)), ((), ()))


def _dot(a, b):
    return jnp.dot(a, b, preferred_element_type=F32)


def _dot_nt(a, b):
    return lax.dot_general(a, b, _NT, preferred_element_type=F32)


def _split3(a):
    hi = a.astype(BF16)
    r1 = a - hi.astype(F32)
    mid = r1.astype(BF16)
    lo = (r1 - mid.astype(F32)).astype(BF16)
    return hi, mid, lo


def _masked_softmax(s, mask):
    sm = jnp.where(mask, s, NEG)
    m = jnp.max(sm, axis=-1, keepdims=True)
    e = jnp.where(mask, jnp.exp(sm - m), 0.0)
    l = jnp.sum(e, axis=-1, keepdims=True)
    return e / jnp.where(l > 0.0, l, 1.0)


def _rms(x, g):
    return x * lax.rsqrt(jnp.mean(x * x, axis=-1, keepdims=True) + EPS) * g


def _silu(x):
    return x * jax.nn.sigmoid(x)


def _const_spec(shape):
    nd = len(shape)
    return pl.BlockSpec(shape, lambda *_: (0,) * nd, pipeline_mode=pl.Buffered(1))


def _params(n_grid):
    return pltpu.CompilerParams(dimension_semantics=("arbitrary",) * n_grid, vmem_limit_bytes=VMEM_LIMIT)


def _proj_kernel(x_ref, g_ref, w_ref, ws_ref, cos_ref, sin_ref, o_ref, os_ref):
    h = _rms(x_ref[...], g_ref[...]).astype(BF16)
    tm = h.shape[0]
    lane = lax.broadcasted_iota(jnp.int32, (tm, LANE), 1)
    first_half = (lane & (B_HD - 1)) < (B_HD // 2)
    cos = cos_ref[...]
    sin = sin_ref[...]
    chunk = 2
    for c0 in range(0, N_COLS, chunk):
        acc = _dot(h, w_ref[:, c0 * LANE:(c0 + chunk) * LANE])
        for j in range(chunk):
            a = acc[:, j * LANE:(j + 1) * LANE]
            if c0 + j in ROPE_COLS:
                swapped = jnp.where(first_half, pltpu.roll(a, LANE - B_HD // 2, 1), pltpu.roll(a, B_HD // 2, 1))
                a = a * cos + swapped * sin
            o_ref[:, (c0 + j) * LANE:(c0 + j + 1) * LANE] = a.astype(BF16)
    os_ref[...] = _dot(h, ws_ref[...])


def _proj(x2, g_pre, w_main, w_small, cos_t, sin_t, seq, tm):
    n = x2.shape[0]
    nq = seq // tm
    return pl.pallas_call(
        _proj_kernel,
        grid=(n // tm,),
        in_specs=[
            pl.BlockSpec((tm, D_MODEL), lambda i: (i, 0)),
            _const_spec((1, D_MODEL)),
            _const_spec((D_MODEL, W_MAIN)),
            _const_spec((D_MODEL, LANE)),
            pl.BlockSpec((tm, LANE), lambda i: (i % nq, 0)),
            pl.BlockSpec((tm, LANE), lambda i: (i % nq, 0)),
        ],
        out_specs=[
            pl.BlockSpec((tm, W_MAIN), lambda i: (i, 0)),
            pl.BlockSpec((tm, LANE), lambda i: (i, 0)),
        ],
        out_shape=[jax.ShapeDtypeStruct((n, W_MAIN), BF16), jax.ShapeDtypeStruct((n, LANE), F32)],
        compiler_params=_params(1),
        name="proj",
    )(x2, g_pre, w_main, w_small, cos_t, sin_t)


def _gmlp_kernel(u_ref, v_ref, z_ref, lng_ref, lnb_ref, ws_ref, bs_ref, o_ref):
    ta = u_ref.shape[0]
    u = jax.nn.gelu(u_ref[...].astype(F32))
    v = jax.nn.gelu(v_ref[...].astype(F32))
    mu = jnp.mean(v, axis=-1, keepdims=True)
    vc = v - mu
    v = vc * lax.rsqrt(jnp.mean(vc * vc, axis=-1, keepdims=True) + EPS) * lng_ref[...] + lnb_ref[...]
    vb = v.astype(BF16)
    gate = _silu(z_ref[...].astype(F32))
    row = lax.broadcasted_iota(jnp.int32, (A_CHUNK, A_CHUNK), 0)
    col = lax.broadcasted_iota(jnp.int32, (A_CHUNK, A_CHUNK), 1)
    tri = col <= row
    for g in range(A_GROUPS):
        w = jnp.where(tri, ws_ref[g], 0.0).astype(BF16)
        b = bs_ref[:, g:g + 1]
        for c in range(ta // A_CHUNK):
            rs = slice(c * A_CHUNK, (c + 1) * A_CHUNK)
            gs = slice(g * LANE, (g + 1) * LANE)
            s = _dot(w, vb[rs, gs]) + b
            o_ref[rs, gs] = (u[rs, gs] * s * gate[rs, gs]).astype(BF16)


def _gmlp(p_main, ln_g, ln_b, ws, bs_t, ta):
    n = p_main.shape[0]
    blk = lambda c: pl.BlockSpec((ta, BRANCH_W), lambda i, c=c: (i, c))
    return pl.pallas_call(
        _gmlp_kernel,
        grid=(n // ta,),
        in_specs=[
            blk(COL_A_U // 4), blk(COL_A_V // 4), blk(COL_A_Z // 4),
            _const_spec((1, BRANCH_W)), _const_spec((1, BRANCH_W)),
            _const_spec((A_GROUPS, A_CHUNK, A_CHUNK)), _const_spec((A_CHUNK, A_GROUPS)),
        ],
        out_specs=pl.BlockSpec((ta, BRANCH_W), lambda i: (i, 0)),
        out_shape=jax.ShapeDtypeStruct((n, BRANCH_W), BF16),
        compiler_params=_params(1),
        name="gmlp",
    )(p_main, p_main, p_main, ln_g, ln_b, ws, bs_t)


def _compress_kernel(x_ref, pos_ref, w1_ref, w2_ref, o_ref):
    x = x_ref[0].astype(F32)
    top = _dot((x + pos_ref[0:1, :]).astype(BF16), w1_ref[0])
    bot = _dot((x + pos_ref[1:2, :]).astype(BF16), w1_ref[1])
    nrow = x.shape[0]
    a = top + pltpu.roll(bot, nrow - 1, 0)
    o_ref[0] = _dot(_silu(a).astype(BF16), w2_ref[...]).astype(BF16)


def _compress(x16, pos2, w1big, w2big):
    bsz, nrow, width = x16.shape
    return pl.pallas_call(
        _compress_kernel,
        grid=(bsz,),
        in_specs=[
            pl.BlockSpec((1, nrow, width), lambda b: (b, 0, 0)),
            _const_spec((2, width)),
            _const_spec((2, width, LANE)),
            _const_spec((LANE, 2 * LANE)),
        ],
        out_specs=pl.BlockSpec((1, nrow, 2 * LANE), lambda b: (b, 0, 0)),
        out_shape=jax.ShapeDtypeStruct((bsz, nrow, 2 * LANE), BF16),
        compiler_params=_params(1),
        name="compress",
    )(x16, pos2, w1big, w2big)


def _nsa_kernel(q_ref, ks_ref, vs_ref, kw_ref, vw_ref, kcmp_ref, vcmp_ref, gl_ref, ovt_ref, e_ref, o_ref,
                m_sc, l_sc, acc_sc, *, tq, kc, seq):
    i = pl.program_id(2)
    qs = i * tq
    scale = B_HD ** -0.5
    nsel = seq // SEL_LEN
    nc = (seq - CMP_LEN) // CMP_STRIDE + 1
    rows = B_GQ * tq

    lane = lax.broadcasted_iota(jnp.int32, (tq, LANE), 1)
    lo = lane < B_HD
    qa = q_ref[:, 0:LANE]
    qb = q_ref[:, LANE:2 * LANE]
    zero = jnp.zeros_like(qa)
    q = jnp.concatenate([jnp.where(lo, qa, zero), jnp.where(lo, zero, qa),
                         jnp.where(lo, qb, zero), jnp.where(lo, zero, qb)], axis=0)
    row = lax.broadcasted_iota(jnp.int32, (rows, 1), 0)
    tpos = qs + (row & (tq - 1))

    s = _dot_nt(q, kcmp_ref[0]) * scale
    cidx = lax.broadcasted_iota(jnp.int32, s.shape, 1)
    ok = ((cidx * CMP_STRIDE + (CMP_LEN - 1)) <= tpos) & (cidx < nc)
    p = _masked_softmax(s, ok)
    o_cmp = _dot(p.astype(BF16), vcmp_ref[0])

    psum = p[0:tq] + p[tq:2 * tq] + p[2 * tq:3 * tq] + p[3 * tq:4 * tq]
    hi = psum.astype(BF16)
    lo_part = (psum - hi.astype(F32)).astype(BF16)
    ovt = ovt_ref[...]
    imp = _dot_nt(ovt, hi) + _dot_nt(ovt, lo_part)
    nselp = imp.shape[0]
    j = lax.broadcasted_iota(jnp.int32, (nselp, tq), 0)
    cur = (qs + lax.broadcasted_iota(jnp.int32, (nselp, tq), 1)) // SEL_LEN
    forced = (j == 0) | (j == cur) | (j == cur - 1)
    imp = jnp.where(forced, BIG, jnp.where(j > cur, NEG, imp))
    imp = jnp.where(j < nsel, imp, -jnp.inf)
    rank = jnp.zeros((nselp, tq), F32)
    for ii in range(nsel):
        r = imp[ii:ii + 1, :]
        beats = (r > imp) | ((r == imp) & (ii < j))
        rank = rank + jnp.where(beats, 1.0, 0.0)
    sel_t = jnp.where((rank < float(min(SEL_N, nsel))) & (j < nsel), 1.0, 0.0)
    if nselp < LANE:
        sel_t = jnp.concatenate([sel_t, jnp.zeros((LANE - nselp, tq), F32)], axis=0)
    sel = sel_t.T.astype(BF16)

    m_sc[...] = jnp.full(m_sc.shape, NEG, F32)
    l_sc[...] = jnp.zeros(l_sc.shape, F32)
    acc_sc[...] = jnp.zeros(acc_sc.shape, F32)

    def body(c, carry):
        k0 = pl.multiple_of(c * kc, kc)
        kk = ks_ref[pl.ds(k0, kc), :]
        vv = vs_ref[pl.ds(k0, kc), :]
        selm = _dot(sel, e_ref[:, pl.ds(k0, kc)])
        kpos = k0 + lax.broadcasted_iota(jnp.int32, (tq, kc), 1)
        tq_pos = qs + lax.broadcasted_iota(jnp.int32, (tq, kc), 0)
        bias = jnp.where((selm > 0.5) & (kpos <= tq_pos), 0.0, NEG)
        sm = _dot_nt(q, kk) * scale + jnp.concatenate([bias] * B_GQ, axis=0)
        m_old = m_sc[...]
        m_new = jnp.maximum(m_old, jnp.max(sm, axis=-1, keepdims=True))
        alpha = jnp.exp(m_old - m_new)
        pe = jnp.where(sm > 0.5 * NEG, jnp.exp(sm - m_new), 0.0)
        l_sc[...] = alpha * l_sc[...] + jnp.sum(pe, axis=-1, keepdims=True)
        acc_sc[...] = alpha * acc_sc[...] + _dot(pe.astype(BF16), vv)
        m_sc[...] = m_new
        return carry

    lax.fori_loop(0, (qs + tq + kc - 1) // kc, body, 0)
    o_sel = acc_sc[...] / l_sc[...]

    wlen = min(WIN + tq, seq)
    w0 = pl.multiple_of(jnp.clip(qs - WIN, 0, seq - wlen), tq)
    wpos = w0 + lax.broadcasted_iota(jnp.int32, (rows, wlen), 1)
    okw = (wpos <= tpos) & (wpos > tpos - WIN)
    pw = _masked_softmax(_dot_nt(q, kw_ref[pl.ds(w0, wlen), :]) * scale, okw)
    o_win = _dot(pw.astype(BF16), vw_ref[pl.ds(w0, wlen), :])

    g = jax.nn.sigmoid(gl_ref[0, 0])
    outs = []
    for r in range(B_GQ):
        rs = slice(r * tq, (r + 1) * tq)
        outs.append(g[:, 3 * r:3 * r + 1] * o_cmp[rs] + g[:, 3 * r + 1:3 * r + 2] * o_sel[rs]
                    + g[:, 3 * r + 2:3 * r + 3] * o_win[rs])
    o_ref[:, 0:LANE] = jnp.where(lo, outs[0], outs[1]).astype(BF16)
    o_ref[:, LANE:2 * LANE] = jnp.where(lo, outs[2], outs[3]).astype(BF16)


def _nsa(p_main, cmp_k, cmp_v, gate_logits, ovt, emat, bsz, seq, tq, kc):
    n = p_main.shape[0]
    nq = seq // tq
    kv_spec = lambda col: pl.BlockSpec((seq, LANE), lambda b, g, i, col=col: (b, col + g))
    nrow = cmp_k.shape[1]
    cmp_spec = pl.BlockSpec((1, nrow, LANE), lambda b, g, i: (b, 0, g))
    rows = B_GQ * tq
    return pl.pallas_call(
        functools.partial(_nsa_kernel, tq=tq, kc=kc, seq=seq),
        grid=(bsz, B_KV, nq),
        in_specs=[
            pl.BlockSpec((tq, 2 * LANE), lambda b, g, i: (b * nq + i, COL_B_Q // 2 + g)),
            kv_spec(COL_KS), kv_spec(COL_VS), kv_spec(COL_KW), kv_spec(COL_VW),
            cmp_spec, cmp_spec,
            pl.BlockSpec((1, 1, tq, 3 * B_GQ), lambda b, g, i: (b, g, i, 0)),
            _const_spec(ovt.shape), _const_spec(emat.shape),
        ],
        out_specs=pl.BlockSpec((tq, 2 * LANE), lambda b, g, i: (b * nq + i, g)),
        out_shape=jax.ShapeDtypeStruct((n, BRANCH_W), BF16),
        scratch_shapes=[pltpu.VMEM((rows, 1), F32), pltpu.VMEM((rows, 1), F32), pltpu.VMEM((rows, LANE), F32)],
        compiler_params=_params(3),
        name="nsa",
    )(p_main, p_main, p_main, p_main, p_main, cmp_k, cmp_v, gate_logits, ovt, emat)


def _fcum_kernel(f_ref, b_ref, u_ref, m_ref, o_ref):
    x = f_ref[0] + b_ref[...]
    lf = jnp.minimum(x, 0.0) - jnp.log(1.0 + jnp.exp(-jnp.abs(x)))
    u = u_ref[...]
    within = sum(_dot(t, u) for t in _split3(lf))
    mm = m_ref[...]
    before = sum(_dot(mm, t) for t in _split3(within))
    o_ref[0] = within + before[:, LANE - 1:LANE]


def _fcum(f_tiles, bias_col, umat, mmat):
    bsz, nrow, _ = f_tiles.shape
    return pl.pallas_call(
        _fcum_kernel,
        grid=(bsz,),
        in_specs=[
            pl.BlockSpec((1, nrow, LANE), lambda b: (b, 0, 0)),
            _const_spec((nrow, 1)), _const_spec((LANE, LANE)), _const_spec((nrow, nrow)),
        ],
        out_specs=pl.BlockSpec((1, nrow, LANE), lambda b: (b, 0, 0)),
        out_shape=jax.ShapeDtypeStruct((bsz, nrow, LANE), F32),
        compiler_params=_params(1),
        name="fcum",
    )(f_tiles, bias_col, umat, mmat)


def _fox_kernel(q_ref, k_ref, v_ref, crow_ref, ccol_ref, o_ref, m_sc, l_sc, acc_sc, *, tq, kc):
    i = pl.program_id(2)
    qs = i * tq
    scale = C_HD ** -0.5
    lane = lax.broadcasted_iota(jnp.int32, (tq, LANE), 1)
    lo = lane < C_HD
    q2 = q_ref[...]
    zero = jnp.zeros_like(q2)
    q = jnp.concatenate([jnp.where(lo, q2, zero), jnp.where(lo, zero, q2)], axis=0)
    ct = jnp.concatenate([ccol_ref[0, 0], ccol_ref[0, 1]], axis=0)

    m_sc[...] = jnp.full(m_sc.shape, NEG, F32)
    l_sc[...] = jnp.zeros(l_sc.shape, F32)
    acc_sc[...] = jnp.zeros(acc_sc.shape, F32)

    def body(c, carry):
        k0 = pl.multiple_of(c * kc, kc)
        kk = k_ref[pl.ds(k0, kc), :]
        vv = v_ref[pl.ds(k0, kc), :]
        kpos = k0 + lax.broadcasted_iota(jnp.int32, (tq, kc), 1)
        tq_pos = qs + lax.broadcasted_iota(jnp.int32, (tq, kc), 0)
        causal = jnp.where(kpos <= tq_pos, 0.0, NEG)
        bias = jnp.concatenate([causal - crow_ref[0, 0, :, pl.ds(k0, kc)],
                                causal - crow_ref[0, 1, :, pl.ds(k0, kc)]], axis=0)
        sm = _dot_nt(q, kk) * scale + ct + bias
        m_old = m_sc[...]
        m_new = jnp.maximum(m_old, jnp.max(sm, axis=-1, keepdims=True))
        alpha = jnp.exp(m_old - m_new)
        pe = jnp.where(sm > 0.5 * NEG, jnp.exp(sm - m_new), 0.0)
        l_sc[...] = alpha * l_sc[...] + jnp.sum(pe, axis=-1, keepdims=True)
        acc_sc[...] = alpha * acc_sc[...] + _dot(pe.astype(BF16), vv)
        m_sc[...] = m_new
        return carry

    lax.fori_loop(0, (qs + tq + kc - 1) // kc, body, 0)
    o = acc_sc[...] / l_sc[...]
    o_ref[...] = jnp.where(lo, o[0:tq], o[tq:2 * tq]).astype(BF16)


def _fox(p_main, c_row, c_col, bsz, seq, tq, kc):
    n = p_main.shape[0]
    nq = seq // tq
    npair = C_HEADS // 2
    return pl.pallas_call(
        functools.partial(_fox_kernel, tq=tq, kc=kc),
        grid=(bsz, npair, nq),
        in_specs=[
            pl.BlockSpec((tq, LANE), lambda b, h, i: (b * nq + i, COL_C_Q + h)),
            pl.BlockSpec((seq, LANE), lambda b, h, i: (b, COL_C_K + h)),
            pl.BlockSpec((seq, LANE), lambda b, h, i: (b, COL_C_V + h)),
            pl.BlockSpec((1, 2, 1, seq), lambda b, h, i: (b, h, 0, 0)),
            pl.BlockSpec((1, 2, tq, 1), lambda b, h, i: (b, h, i, 0)),
        ],
        out_specs=pl.BlockSpec((tq, LANE), lambda b, h, i: (b * nq + i, h)),
        out_shape=jax.ShapeDtypeStruct((n, BRANCH_W), BF16),
        scratch_shapes=[pltpu.VMEM((2 * tq, 1), F32), pltpu.VMEM((2 * tq, 1), F32), pltpu.VMEM((2 * tq, LANE), F32)],
        compiler_params=_params(3),
        name="fox",
    )(p_main, p_main, p_main, c_row, c_col)


def _memkv_kernel(mem_ref, g_ref, w_ref, o_ref):
    h = _rms(mem_ref[0], g_ref[...]).astype(BF16)
    o_ref[0] = _dot(h, w_ref[...]).astype(BF16)


def _memkv(mem, g_mem, w_kv):
    bsz, mlen, _ = mem.shape
    return pl.pallas_call(
        _memkv_kernel,
        grid=(bsz,),
        in_specs=[
            pl.BlockSpec((1, mlen, D_MODEL), lambda b: (b, 0, 0)),
            _const_spec((1, D_MODEL)), _const_spec((D_MODEL, 2 * BRANCH_W)),
        ],
        out_specs=pl.BlockSpec((1, mlen, 2 * BRANCH_W), lambda b: (b, 0, 0)),
        out_shape=jax.ShapeDtypeStruct((bsz, mlen, 2 * BRANCH_W), BF16),
        compiler_params=_params(1),
        name="memkv",
    )(mem, g_mem, w_kv)


def _merge_kernel(x_ref, gpre_ref, gpost_ref, ya_ref, yb_ref, yc_ref, bz_ref, cz_ref, mq_ref, mz_ref, mkv_ref,
                  wg_ref, wbr_ref, wo_ref, o_ref):
    x = x_ref[...]
    h = _rms(x, gpre_ref[...]).astype(BF16)

    mq = mq_ref[...]
    scale = M_HD ** -0.5
    ym = []
    for hd in range(M_HEADS):
        mk = mkv_ref[0, :, hd * M_HD:(hd + 1) * M_HD]
        mv = mkv_ref[0, :, BRANCH_W + hd * M_HD:BRANCH_W + (hd + 1) * M_HD]
        s = _dot_nt(mq[:, hd * M_HD:(hd + 1) * M_HD], mk) * scale
        e = jnp.exp(s - jnp.max(s, axis=-1, keepdims=True))
        p = e / jnp.sum(e, axis=-1, keepdims=True)
        ym.append(_dot(p.astype(BF16), mv))
    y_m = jnp.concatenate(ym, axis=-1)

    ys = (
        ya_ref[...],
        (yb_ref[...].astype(F32) * _silu(bz_ref[...].astype(F32))).astype(BF16),
        (yc_ref[...].astype(F32) * _silu(cz_ref[...].astype(F32))).astype(BF16),
        (y_m * _silu(mz_ref[...].astype(F32))).astype(BF16),
    )
    merged = None
    for nb in range(N_BRANCH):
        gate = jax.nn.sigmoid(_dot(h, wg_ref[:, nb * D_MODEL:(nb + 1) * D_MODEL]))
        term = gate * _dot(ys[nb], wbr_ref[nb])
        merged = term if merged is None else merged + term
    out = _dot(merged.astype(BF16), wo_ref[...])
    o_ref[...] = x + _rms(out, gpost_ref[...])


def _merge(x2, g_pre, g_post, y_a, y_b, y_c, p_main, mkv, w_gate, w_br, w_o, seq, tz):
    n = x2.shape[0]
    nq = seq // tz
    mlen = mkv.shape[1]
    act = lambda c: pl.BlockSpec((tz, BRANCH_W), lambda i, c=c: (i, c))
    return pl.pallas_call(
        _merge_kernel,
        grid=(n // tz,),
        in_specs=[
            pl.BlockSpec((tz, D_MODEL), lambda i: (i, 0)),
            _const_spec((1, D_MODEL)), _const_spec((1, D_MODEL)),
            act(0), act(0), act(0),
            act(COL_B_Z // 4), act(COL_C_Z // 4), act(COL_M_Q // 4), act(COL_M_Z // 4),
            pl.BlockSpec((1, mlen, 2 * BRANCH_W), lambda i: (i // nq, 0, 0)),
            _const_spec((D_MODEL, N_BRANCH * D_MODEL)),
            _const_spec((N_BRANCH, BRANCH_W, D_MODEL)),
            _const_spec((D_MODEL, D_MODEL)),
        ],
        out_specs=pl.BlockSpec((tz, D_MODEL), lambda i: (i, 0)),
        out_shape=jax.ShapeDtypeStruct((n, D_MODEL), F32),
        compiler_params=_params(1),
        name="merge",
    )(x2, g_pre, g_post, y_a, y_b, y_c, p_main, p_main, p_main, p_main, mkv, w_gate, w_br, w_o)


def _pack_w_in(w):
    (a_u, a_v, a_z, b_q, b_kc, b_vc, b_ks, b_vs, b_kw, b_vw, b_g, b_z,
     c_q, c_k, c_v, c_f, c_z, m_q, m_z) = jnp.split(w, IN_SPLITS, axis=1)
    dup = lambda t: jnp.concatenate([t[:, :B_HD], t[:, :B_HD], t[:, B_HD:], t[:, B_HD:]], axis=1)
    main = jnp.concatenate([a_u, a_v, a_z, b_q, b_z, c_q, c_k, c_v, c_z, m_q, m_z,
                            dup(b_ks), dup(b_kw), dup(b_vs), dup(b_vw), b_kc, b_vc], axis=1)
    small = jnp.concatenate([b_g, c_f, jnp.zeros((w.shape[0], LANE - b_g.shape[1] - c_f.shape[1]), w.dtype)], axis=1)
    return main.astype(BF16), small.astype(BF16)


def _pack_compress(pos, w1, w2):
    eye = jnp.eye(B_KV, dtype=F32)
    w1r = w1.reshape(2, CMP_STRIDE, B_HD, B_HD)
    w1big = jnp.einsum('hlde,gk->hlgdke', w1r, eye).reshape(2, CMP_STRIDE * B_KV * B_HD, B_KV * B_HD)
    w2big = jnp.einsum('de,gk,u->gdkue', w2, eye, jnp.ones((2,), F32)).reshape(B_KV * B_HD, 2 * B_KV * B_HD)
    pos2 = jnp.broadcast_to(pos.reshape(2, CMP_STRIDE, 1, B_HD), (2, CMP_STRIDE, B_KV, B_HD)).reshape(2, -1)
    return pos2.astype(F32), w1big.astype(BF16), w2big.astype(BF16)


def _rope_tables(seq):
    half = B_HD // 2
    freqs = ROPE_THETA ** (-jnp.arange(half, dtype=F32) / half)
    ang = jnp.arange(seq, dtype=F32)[:, None] * freqs[None, :]
    cos, sin = jnp.cos(ang), jnp.sin(ang)
    cos_t = jnp.concatenate([cos, cos, cos, cos], axis=1)
    sin_t = jnp.concatenate([-sin, sin, -sin, sin], axis=1)
    return cos_t, sin_t


def _nsa_constants(seq):
    nc = (seq - CMP_LEN) // CMP_STRIDE + 1
    nsel = seq // SEL_LEN
    nselp = -(-nsel // 8) * 8
    ncp = seq // CMP_STRIDE
    ci = np.arange(ncp) * CMP_STRIDE
    sj = np.arange(nselp) * SEL_LEN
    ovt = ((ci[None, :] <= sj[:, None] + SEL_LEN - 1) & (ci[None, :] + CMP_LEN - 1 >= sj[:, None])
           & (np.arange(ncp)[None, :] < nc) & (np.arange(nselp)[:, None] < nsel))
    emat = (np.arange(seq)[None, :] // SEL_LEN) == np.arange(LANE)[:, None]
    return jnp.asarray(ovt, BF16), jnp.asarray(emat, BF16)


def _fcum_constants(seq):
    ntile = seq // LANE
    nrow = C_HEADS * ntile
    umat = np.arange(LANE)[:, None] <= np.arange(LANE)[None, :]
    r = np.arange(nrow)
    mmat = (r[:, None] // ntile == r[None, :] // ntile) & (r[None, :] < r[:, None])
    return jnp.asarray(umat, BF16), jnp.asarray(mmat, BF16)


def _layer(x2, mem, bsz, seq, tables, w_in, g_pre, g_post, g_mem, w_mem_kv, a_ln_g, a_ln_b, a_ws, a_bs,
           pos_k, w1_k, w2_k, pos_v, w1_v, w2_v, c_fbias, w_br, w_gate, w_o):
    cos_t, sin_t, ovt, emat, umat, mmat = tables
    n = bsz * seq
    w_main, w_small = _pack_w_in(w_in)
    p_main, p_small = _proj(x2, g_pre.reshape(1, -1), w_main, w_small, cos_t, sin_t, seq, tm=min(512, seq))

    y_a = _gmlp(p_main, a_ln_g.reshape(1, -1), a_ln_b.reshape(1, -1), a_ws, a_bs.T, ta=min(512, seq))

    to16 = lambda col: p_main[:, col * LANE:(col + 1) * LANE].reshape(bsz, seq // CMP_STRIDE, CMP_STRIDE * LANE)
    cmp_k = _compress(to16(COL_KC), *_pack_compress(pos_k, w1_k, w2_k))
    cmp_v = _compress(to16(COL_VC), *_pack_compress(pos_v, w1_v, w2_v))
    gl = p_small[:, SMALL_G0:SMALL_G0 + 3 * B_HEADS].reshape(bsz, seq, B_KV, 3 * B_GQ).transpose(0, 2, 1, 3)
    y_b = _nsa(p_main, cmp_k, cmp_v, gl, ovt, emat, bsz, seq, tq=128, kc=min(512, seq))

    ntile = seq // LANE
    f_tiles = (p_small[:, SMALL_F0:SMALL_F0 + C_HEADS].reshape(bsz, seq, C_HEADS).transpose(0, 2, 1)
               .reshape(bsz, C_HEADS * ntile, LANE))
    bias_col = jnp.repeat(c_fbias.astype(F32), ntile).reshape(-1, 1)
    cum = _fcum(f_tiles, bias_col, umat, mmat).reshape(bsz, C_HEADS, seq)
    y_c = _fox(p_main, cum[:, :, None, :], cum[:, :, :, None], bsz, seq, tq=min(256, seq), kc=min(256, seq))

    mkv = _memkv(mem, g_mem.reshape(1, -1), w_mem_kv.astype(BF16))
    wg = w_gate.reshape(D_MODEL, N_BRANCH * D_MODEL).astype(BF16)
    return _merge(x2, g_pre.reshape(1, -1), g_post.reshape(1, -1), y_a, y_b, y_c, p_main, mkv,
                  wg, w_br.astype(BF16), w_o.astype(BF16), seq, tz=min(256, seq))


def kernel(x, mem, w_in, g_pre, g_post, g_mem, w_mem_kv, a_ln_g, a_ln_b, a_ws, a_bs, b_cmp_pos_k, b_cmp_w1_k,
           b_cmp_w2_k, b_cmp_pos_v, b_cmp_w1_v, b_cmp_w2_v, c_fbias, w_br, w_gate, w_o):
    bsz, seq, d = x.shape
    tables = _rope_tables(seq) + _nsa_constants(seq) + _fcum_constants(seq)
    x2 = x.reshape(bsz * seq, d)
    for l in range(w_in.shape[0]):
        x2 = _layer(x2, mem, bsz, seq, tables, w_in[l], g_pre[l], g_post[l], g_mem[l], w_mem_kv[l],
                    a_ln_g[l], a_ln_b[l], a_ws[l], a_bs[l], b_cmp_pos_k[l], b_cmp_w1_k[l], b_cmp_w2_k[l],
                    b_cmp_pos_v[l], b_cmp_w1_v[l], b_cmp_w2_v[l], c_fbias[l], w_br[l], w_gate[l], w_o[l])
    return x2.reshape(bsz, seq, d)
```

```python
import functools

import numpy as np
import jax
import jax.numpy as jnp
from jax import lax
from jax.experimental import pallas as pl
from jax.experimental.pallas import tpu as pltpu

F32 = jnp.float32
BF16 = jnp.bfloat16

D_MODEL = 1024
N_BRANCH = 4
BRANCH_W = 512
A_GROUPS = 4
A_CHUNK = 128
B_HEADS = 8
B_KV = 2
B_GQ = B_HEADS // B_KV
B_HD = 64
CMP_LEN = 32
CMP_STRIDE = 16
SEL_LEN = 64
SEL_N = 8
WIN = 256
C_HEADS = 8
C_HD = 64
M_HEADS = 4
M_HD = BRANCH_W // M_HEADS
ROPE_THETA = 10000.0
EPS = 1e-6
NEG = -1e30
BIG = 1e9

IN_SIZES = (
    BRANCH_W, BRANCH_W, BRANCH_W,
    B_HEADS * B_HD, B_KV * B_HD, B_KV * B_HD, B_KV * B_HD,
    B_KV * B_HD, B_KV * B_HD, B_KV * B_HD, B_HEADS * 3, BRANCH_W,
    C_HEADS * C_HD, C_HEADS * C_HD, C_HEADS * C_HD, C_HEADS, BRANCH_W,
    M_HEADS * M_HD, BRANCH_W,
)
IN_SPLITS = tuple(int(c) for c in np.cumsum(IN_SIZES)[:-1])

LANE = 128
VMEM_LIMIT = 56 * 1024 * 1024

NSA_TQ = 128
FOX_TQ = 256
FOX_PAIRS = 2

COL_A_U, COL_A_V, COL_A_Z = 0, 4, 8
COL_B_Q, COL_B_Z = 12, 16
COL_C_Q, COL_C_K, COL_C_V, COL_C_Z = 20, 24, 28, 32
COL_M_Q, COL_M_Z = 36, 40
COL_KS, COL_KW, COL_VS, COL_VW = 44, 46, 48, 50
COL_KC, COL_VC = 52, 53
N_COLS = 54
ROPE_COLS = frozenset(list(range(COL_B_Q, COL_B_Q + 4)) + [COL_KS, COL_KS + 1, COL_KW, COL_KW + 1, COL_KC])
W_MAIN = N_COLS * LANE
SMALL_G0, SMALL_F0 = 0, B_HEADS * 3

_NT = (((1,), (1,)), ((), ()))
_TN = (((0,), (0,)), ((), ()))


def _dot(a, b):
    return jnp.dot(a, b, preferred_element_type=F32)


def _dot_nt(a, b):
    return lax.dot_general(a, b, _NT, preferred_element_type=F32)


def _dot_tn(a, b):
    return lax.dot_general(a, b, _TN, preferred_element_type=F32)


def _online_steps(sts, vs, m_sc, l_sc, acc_sc):
    pes, alphas = [], []
    for g, st in enumerate(sts):
        m_old = m_sc[g]
        m_new = jnp.maximum(m_old, jnp.max(st, axis=0, keepdims=True))
        alpha = jnp.exp(m_old - m_new)
        pe = jnp.exp(st - m_new)
        l_sc[g] = alpha * l_sc[g] + jnp.sum(pe, axis=0, keepdims=True)
        m_sc[g] = m_new
        pes.append(pe.astype(BF16))
        alphas.append(alpha)
    for g, (pe, alpha) in enumerate(zip(pes, alphas)):
        for part, (v, cols) in enumerate(vs[g]):
            acc_sc[g, part] = alpha[:, cols] * acc_sc[g, part] + _dot_tn(v, pe[:, cols])


def _split3(a):
    hi = a.astype(BF16)
    r1 = a - hi.astype(F32)
    mid = r1.astype(BF16)
    lo = (r1 - mid.astype(F32)).astype(BF16)
    return hi, mid, lo


def _rms(x, g):
    return x * lax.rsqrt(jnp.mean(x * x, axis=-1, keepdims=True) + EPS) * g


def _silu(x):
    return x * jax.nn.sigmoid(x)


def _const_spec(shape):
    nd = len(shape)
    return pl.BlockSpec(shape, lambda *_: (0,) * nd, pipeline_mode=pl.Buffered(1))


def _params(n_grid):
    return pltpu.CompilerParams(dimension_semantics=("arbitrary",) * n_grid, vmem_limit_bytes=VMEM_LIMIT)


def _proj_kernel(x_ref, g_ref, w_ref, ws_ref, cos_ref, sin_ref, o_ref, os_ref):
    h = _rms(x_ref[...], g_ref[...]).astype(BF16)
    tm = h.shape[0]
    lane = lax.broadcasted_iota(jnp.int32, (tm, LANE), 1)
    first_half = (lane & (B_HD - 1)) < (B_HD // 2)
    cos = cos_ref[...]
    sin = sin_ref[...]
    chunk = 2
    for c0 in range(0, N_COLS, chunk):
        acc = _dot(h, w_ref[:, c0 * LANE:(c0 + chunk) * LANE])
        for j in range(chunk):
            a = acc[:, j * LANE:(j + 1) * LANE]
            if c0 + j in ROPE_COLS:
                swapped = jnp.where(first_half, pltpu.roll(a, LANE - B_HD // 2, 1), pltpu.roll(a, B_HD // 2, 1))
                a = a * cos + swapped * sin
            o_ref[:, (c0 + j) * LANE:(c0 + j + 1) * LANE] = a.astype(BF16)
    os_ref[...] = _dot(h, ws_ref[...])


def _proj(x2, g_pre, w_main, w_small, cos_t, sin_t, seq, tm):
    n = x2.shape[0]
    nq = seq // tm
    return pl.pallas_call(
        _proj_kernel,
        grid=(n // tm,),
        in_specs=[
            pl.BlockSpec((tm, D_MODEL), lambda i: (i, 0)),
            _const_spec((1, D_MODEL)),
            _const_spec((D_MODEL, W_MAIN)),
            _const_spec((D_MODEL, LANE)),
            pl.BlockSpec((tm, LANE), lambda i: (i % nq, 0)),
            pl.BlockSpec((tm, LANE), lambda i: (i % nq, 0)),
        ],
        out_specs=[
            pl.BlockSpec((tm, W_MAIN), lambda i: (i, 0)),
            pl.BlockSpec((tm, LANE), lambda i: (i, 0)),
        ],
        out_shape=[jax.ShapeDtypeStruct((n, W_MAIN), BF16), jax.ShapeDtypeStruct((n, LANE), F32)],
        compiler_params=_params(1),
        name="proj",
    )(x2, g_pre, w_main, w_small, cos_t, sin_t)


def _gmlp_kernel(u_ref, v_ref, z_ref, lng_ref, lnb_ref, ws_ref, bs_ref, o_ref):
    ta = u_ref.shape[0]
    u = jax.nn.gelu(u_ref[...].astype(F32))
    v = jax.nn.gelu(v_ref[...].astype(F32))
    mu = jnp.mean(v, axis=-1, keepdims=True)
    vc = v - mu
    v = vc * lax.rsqrt(jnp.mean(vc * vc, axis=-1, keepdims=True) + EPS) * lng_ref[...] + lnb_ref[...]
    vb = v.astype(BF16)
    gate = _silu(z_ref[...].astype(F32))
    row = lax.broadcasted_iota(jnp.int32, (A_CHUNK, A_CHUNK), 0)
    col = lax.broadcasted_iota(jnp.int32, (A_CHUNK, A_CHUNK), 1)
    tri = col <= row
    for g in range(A_GROUPS):
        w = jnp.where(tri, ws_ref[g], 0.0).astype(BF16)
        b = bs_ref[:, g:g + 1]
        for c in range(ta // A_CHUNK):
            rs = slice(c * A_CHUNK, (c + 1) * A_CHUNK)
            gs = slice(g * LANE, (g + 1) * LANE)
            s = _dot(w, vb[rs, gs]) + b
            o_ref[rs, gs] = (u[rs, gs] * s * gate[rs, gs]).astype(BF16)


def _gmlp(p_main, ln_g, ln_b, ws, bs_t, ta):
    n = p_main.shape[0]
    blk = lambda c: pl.BlockSpec((ta, BRANCH_W), lambda i, c=c: (i, c))
    return pl.pallas_call(
        _gmlp_kernel,
        grid=(n // ta,),
        in_specs=[
            blk(COL_A_U // 4), blk(COL_A_V // 4), blk(COL_A_Z // 4),
            _const_spec((1, BRANCH_W)), _const_spec((1, BRANCH_W)),
            _const_spec((A_GROUPS, A_CHUNK, A_CHUNK)), _const_spec((A_CHUNK, A_GROUPS)),
        ],
        out_specs=pl.BlockSpec((ta, BRANCH_W), lambda i: (i, 0)),
        out_shape=jax.ShapeDtypeStruct((n, BRANCH_W), BF16),
        compiler_params=_params(1),
        name="gmlp",
    )(p_main, p_main, p_main, ln_g, ln_b, ws, bs_t)


def _compress_kernel(x_ref, pos_ref, w1_ref, w2_ref, o_ref):
    x = x_ref[0].astype(F32)
    top = _dot((x + pos_ref[0:1, :]).astype(BF16), w1_ref[0])
    bot = _dot((x + pos_ref[1:2, :]).astype(BF16), w1_ref[1])
    nrow = x.shape[0]
    a = top + pltpu.roll(bot, nrow - 1, 0)
    o_ref[0] = _dot(_silu(a).astype(BF16), w2_ref[...]).astype(BF16)


def _compress(x16, pos2, w1big, w2big):
    bsz, nrow, width = x16.shape
    return pl.pallas_call(
        _compress_kernel,
        grid=(bsz,),
        in_specs=[
            pl.BlockSpec((1, nrow, width), lambda b: (b, 0, 0)),
            _const_spec((2, width)),
            _const_spec((2, width, LANE)),
            _const_spec((LANE, 2 * LANE)),
        ],
        out_specs=pl.BlockSpec((1, nrow, 2 * LANE), lambda b: (b, 0, 0)),
        out_shape=jax.ShapeDtypeStruct((bsz, nrow, 2 * LANE), BF16),
        compiler_params=_params(1),
        name="compress",
    )(x16, pos2, w1big, w2big)


def _softmax_t(st, mask):
    sm = jnp.where(mask, st, NEG)
    m = jnp.max(sm, axis=0, keepdims=True)
    e = jnp.where(mask, jnp.exp(sm - m), 0.0)
    return e, jnp.sum(e, axis=0, keepdims=True)


def _nsa_kernel(q_ref, ks_ref, vs_ref, kw_ref, vw_ref, kcmp_ref, vcmp_ref, glt_ref, ovt_ref, cb_ref, wb_ref, o_ref,
                bias_sc, m_sc, l_sc, acc_sc, *, tq, seq):
    i = pl.program_id(1)
    qs = i * tq
    kc = 2 * tq
    scale = B_HD ** -0.5
    nsel = seq // SEL_LEN
    nc = (seq - CMP_LEN) // CMP_STRIDE + 1
    nlane = B_GQ * tq
    lane = lax.broadcasted_iota(jnp.int32, (tq, LANE), 1)
    lo = lane < B_HD
    tpos = qs + (lax.broadcasted_iota(jnp.int32, (1, nlane), 1) & (tq - 1))
    ovt = ovt_ref[...]
    nselp = ovt.shape[0]
    j = lax.broadcasted_iota(jnp.int32, (nselp, tq), 0)
    cur = (qs + lax.broadcasted_iota(jnp.int32, (nselp, tq), 1)) // SEL_LEN
    forced = (j == 0) | (j == cur) | (j == cur - 1)
    future = j > cur

    qg, o_cmp = [], []
    for g in range(B_KV):
        qa = q_ref[:, 2 * g * LANE:(2 * g + 1) * LANE].astype(F32) * scale
        qb = q_ref[:, (2 * g + 1) * LANE:(2 * g + 2) * LANE].astype(F32) * scale
        q = jnp.concatenate([jnp.where(lo, qa, 0.0), jnp.where(lo, 0.0, qa),
                             jnp.where(lo, qb, 0.0), jnp.where(lo, 0.0, qb)], axis=0)
        q = q.T.astype(BF16)
        qg.append(q)

        st = _dot(kcmp_ref[0, :, g * LANE:(g + 1) * LANE], q)
        cpos = lax.broadcasted_iota(jnp.int32, st.shape, 0)
        ok = ((cpos * CMP_STRIDE + (CMP_LEN - 1)) <= tpos) & (cpos < nc)
        e, l = _softmax_t(st, ok)
        p = e / jnp.where(l > 0.0, l, 1.0)
        o_cmp.append(_dot_tn(vcmp_ref[0, :, g * LANE:g * LANE + B_HD], p.astype(BF16)))

        psum = p[:, 0:tq] + p[:, tq:2 * tq] + p[:, 2 * tq:3 * tq] + p[:, 3 * tq:4 * tq]
        hi = psum.astype(BF16)
        lo_part = (psum - hi.astype(F32)).astype(BF16)
        imp = _dot(ovt, hi) + _dot(ovt, lo_part)
        imp = jnp.where(forced, BIG, jnp.where(future, NEG, imp))
        imp = jnp.where(j < nsel, imp, -jnp.inf)
        rank = jnp.zeros((nselp, tq), F32)
        for ii in range(nsel):
            r = imp[ii:ii + 1, :]
            beats = (r > imp) | ((r == imp) & (ii < j))
            rank = rank + jnp.where(beats, 1.0, 0.0)
        bias_t = jnp.where((rank < float(min(SEL_N, nsel))) & (j < nsel), 0.0, NEG)
        bias_sc[g] = jnp.concatenate([bias_t] * B_GQ, axis=1)

    m_sc[...] = jnp.full(m_sc.shape, NEG, F32)
    l_sc[...] = jnp.zeros(l_sc.shape, F32)
    acc_sc[...] = jnp.zeros(acc_sc.shape, F32)

    def step(k0, klen, diagonal):
        sts = [_dot(ks_ref[pl.ds(k0, klen), g * LANE:(g + 1) * LANE], qg[g]) for g in range(B_KV)]
        for g in range(B_KV):
            if diagonal:
                sts[g] = sts[g] + cb_ref[...]
            else:
                j0 = k0 // SEL_LEN
                sts[g] = sts[g] + jnp.concatenate(
                    [jnp.broadcast_to(bias_sc[g, pl.ds(j0 + jb, 1), :], (SEL_LEN, nlane))
                     for jb in range(klen // SEL_LEN)], axis=0)
        vs = [[(vs_ref[pl.ds(k0, klen), g * LANE:g * LANE + B_HD], slice(None))] for g in range(B_KV)]
        _online_steps(sts, vs, m_sc, l_sc, acc_sc)

    def body(c, carry):
        step(pl.multiple_of(c * kc, kc), kc, False)
        return carry

    lax.fori_loop(0, qs // kc, body, 0)

    @pl.when(qs % kc != 0)
    def _():
        step(pl.multiple_of(qs - tq, tq), tq, False)

    step(pl.multiple_of(qs, tq), tq, True)

    wlen = min(WIN + tq, seq)
    w0 = pl.multiple_of(jnp.clip(qs - WIN, 0, seq - wlen), tq)
    wb = wb_ref[pl.ds(pl.multiple_of(WIN - (qs - w0), tq), wlen), :]
    g_all = jax.nn.sigmoid(glt_ref[0])
    for g in range(B_KV):
        st = _dot(kw_ref[pl.ds(w0, wlen), g * LANE:(g + 1) * LANE], qg[g]) + wb
        e = jnp.exp(st - jnp.max(st, axis=0, keepdims=True))
        o_win = _dot_tn(vw_ref[pl.ds(w0, wlen), g * LANE:g * LANE + B_HD], e.astype(BF16))
        o_win = o_win / jnp.sum(e, axis=0, keepdims=True)
        o_sel = acc_sc[g, 0] / l_sc[g]

        gt = g_all[g]
        outs = []
        for r in range(B_GQ):
            cs = slice(r * tq, (r + 1) * tq)
            outs.append(gt[3 * r:3 * r + 1, :] * o_cmp[g][:, cs] + gt[3 * r + 1:3 * r + 2, :] * o_sel[:, cs]
                        + gt[3 * r + 2:3 * r + 3, :] * o_win[:, cs])
        o_ref[:, 2 * g * LANE:(2 * g + 1) * LANE] = jnp.concatenate(outs[0:2], axis=0).T.astype(BF16)
        o_ref[:, (2 * g + 1) * LANE:(2 * g + 2) * LANE] = jnp.concatenate(outs[2:4], axis=0).T.astype(BF16)


def _nsa(p_main, cmp_k, cmp_v, gate_logits_t, ovt, causal_bias, window_bias, bsz, seq, tq):
    n = p_main.shape[0]
    nq = seq // tq
    kv_spec = lambda col: pl.BlockSpec((seq, B_KV * LANE), lambda b, i, col=col: (b, col // B_KV))
    nrow = cmp_k.shape[1]
    cmp_spec = pl.BlockSpec((1, nrow, B_KV * LANE), lambda b, i: (b, 0, 0))
    nlane = B_GQ * tq
    return pl.pallas_call(
        functools.partial(_nsa_kernel, tq=tq, seq=seq),
        grid=(bsz, nq),
        in_specs=[
            pl.BlockSpec((tq, BRANCH_W), lambda b, i: (b * nq + i, COL_B_Q // 4)),
            kv_spec(COL_KS), kv_spec(COL_VS), kv_spec(COL_KW), kv_spec(COL_VW),
            cmp_spec, cmp_spec,
            pl.BlockSpec((1, B_KV, 3 * B_GQ, tq), lambda b, i: (b, 0, 0, i)),
            _const_spec(ovt.shape), _const_spec(causal_bias.shape), _const_spec(window_bias.shape),
        ],
        out_specs=pl.BlockSpec((tq, BRANCH_W), lambda b, i: (b * nq + i, 0)),
        out_shape=jax.ShapeDtypeStruct((n, BRANCH_W), BF16),
        scratch_shapes=[pltpu.VMEM((B_KV, ovt.shape[0], nlane), F32), pltpu.VMEM((B_KV, 1, nlane), F32),
                        pltpu.VMEM((B_KV, 1, nlane), F32), pltpu.VMEM((B_KV, 1, B_HD, nlane), F32)],
        compiler_params=_params(2),
        name="nsa",
    )(p_main, p_main, p_main, p_main, p_main, cmp_k, cmp_v, gate_logits_t, ovt, causal_bias, window_bias)


def _fcum_kernel(f_ref, b_ref, u_ref, m_ref, o_ref, hi_ref, mid_ref, lo_ref):
    x = f_ref[0] + b_ref[...]
    lf = jnp.minimum(x, 0.0) - jnp.log(1.0 + jnp.exp(-jnp.abs(x)))
    u = u_ref[...]
    within = sum(_dot(t, u) for t in _split3(lf))
    mm = m_ref[...]
    before = sum(_dot(mm, t) for t in _split3(within))
    c = within + before[:, LANE - 1:LANE]
    o_ref[0] = c
    hi_ref[0], mid_ref[0], lo_ref[0] = _split3(c)


def _fcum(f_tiles, bias_col, umat, mmat):
    bsz, nrow, _ = f_tiles.shape
    spec = pl.BlockSpec((1, nrow, LANE), lambda b: (b, 0, 0))
    return pl.pallas_call(
        _fcum_kernel,
        grid=(bsz,),
        in_specs=[spec, _const_spec((nrow, 1)), _const_spec((LANE, LANE)), _const_spec((nrow, nrow))],
        out_specs=[spec] * 4,
        out_shape=[jax.ShapeDtypeStruct((bsz, nrow, LANE), F32)] + [jax.ShapeDtypeStruct((bsz, nrow, LANE), BF16)] * 3,
        compiler_params=_params(1),
        name="fcum",
    )(f_tiles, bias_col, umat, mmat)


def _fox_kernel(q_ref, k_ref, v_ref, kaug_ref, ccol_ref, cb_ref, o_ref, m_sc, l_sc, acc_sc, *, tq):
    i = pl.program_id(2)
    kc = tq
    scale = C_HD ** -0.5
    lane = lax.broadcasted_iota(jnp.int32, (tq, LANE), 1)
    lo = lane < C_HD
    qxs = []
    for pr in range(FOX_PAIRS):
        q2 = q_ref[:, pr * LANE:(pr + 1) * LANE].astype(F32) * scale
        halves = []
        for hd in range(2):
            hi, mid, lw = _split3(ccol_ref[0, 2 * pr + hd])
            aug = jnp.where(lane == 0, hi.astype(F32), jnp.where(lane == 1, mid.astype(F32), jnp.where(
                lane == 2, lw.astype(F32), jnp.where((lane >= 3 + 3 * hd) & (lane < 6 + 3 * hd), -1.0, 0.0))))
            qh = jnp.where(lo, q2, 0.0) if hd == 0 else jnp.where(lo, 0.0, q2)
            halves.append(jnp.concatenate([qh, aug], axis=1))
        qxs.append(jnp.concatenate(halves, axis=0).T.astype(BF16))

    m_sc[...] = jnp.full(m_sc.shape, NEG, F32)
    l_sc[...] = jnp.zeros(l_sc.shape, F32)
    acc_sc[...] = jnp.zeros(acc_sc.shape, F32)

    def scores(c):
        k0 = pl.multiple_of(c * kc, kc)
        sts = []
        for pr in range(FOX_PAIRS):
            cols = slice(pr * LANE, (pr + 1) * LANE)
            kx = jnp.concatenate([k_ref[pl.ds(k0, kc), cols], kaug_ref[0, pr, pl.ds(k0, kc), :]], axis=1)
            sts.append(_dot(kx, qxs[pr]))
        return tuple(sts)

    def consume(c, sts):
        k0 = pl.multiple_of(c * kc, kc)
        vs = [[(v_ref[pl.ds(k0, kc), pr * LANE + hd * C_HD:pr * LANE + (hd + 1) * C_HD],
                slice(hd * tq, (hd + 1) * tq)) for hd in range(2)] for pr in range(FOX_PAIRS)]
        _online_steps(sts, vs, m_sc, l_sc, acc_sc)

    def body(c, carry):
        consume(c, scores(c))
        return carry

    lax.fori_loop(0, i, body, 0)
    consume(i, [st + cb_ref[...] for st in scores(i)])

    for pr in range(FOX_PAIRS):
        l = l_sc[pr]
        ot = jnp.concatenate([acc_sc[pr, 0] / l[:, 0:tq], acc_sc[pr, 1] / l[:, tq:2 * tq]], axis=0)
        o_ref[:, pr * LANE:(pr + 1) * LANE] = ot.T.astype(BF16)


def _fox(p_main, k_aug, c_col, causal_bias, bsz, seq, tq):
    n = p_main.shape[0]
    nq = seq // tq
    ngrp = C_HEADS // 2 // FOX_PAIRS
    wide = FOX_PAIRS * LANE
    return pl.pallas_call(
        functools.partial(_fox_kernel, tq=tq),
        grid=(bsz, ngrp, nq),
        in_specs=[
            pl.BlockSpec((tq, wide), lambda b, h, i: (b * nq + i, COL_C_Q // FOX_PAIRS + h)),
            pl.BlockSpec((seq, wide), lambda b, h, i: (b, COL_C_K // FOX_PAIRS + h)),
            pl.BlockSpec((seq, wide), lambda b, h, i: (b, COL_C_V // FOX_PAIRS + h)),
            pl.BlockSpec((1, FOX_PAIRS, seq, LANE), lambda b, h, i: (b, h, 0, 0)),
            pl.BlockSpec((1, 2 * FOX_PAIRS, tq, 1), lambda b, h, i: (b, h, i, 0)),
            _const_spec(causal_bias.shape),
        ],
        out_specs=pl.BlockSpec((tq, wide), lambda b, h, i: (b * nq + i, h)),
        out_shape=jax.ShapeDtypeStruct((n, BRANCH_W), BF16),
        scratch_shapes=[pltpu.VMEM((FOX_PAIRS, 1, 2 * tq), F32), pltpu.VMEM((FOX_PAIRS, 1, 2 * tq), F32),
                        pltpu.VMEM((FOX_PAIRS, 2, C_HD, tq), F32)],
        compiler_params=_params(3),
        name="fox",
    )(p_main, p_main, p_main, k_aug, c_col, causal_bias)


def _memkv_kernel(mem_ref, g_ref, w_ref, o_ref):
    h = _rms(mem_ref[0], g_ref[...]).astype(BF16)
    o_ref[0] = _dot(h, w_ref[...]).astype(BF16)


def _memkv(mem, g_mem, w_kv):
    bsz, mlen, _ = mem.shape
    return pl.pallas_call(
        _memkv_kernel,
        grid=(bsz,),
        in_specs=[
            pl.BlockSpec((1, mlen, D_MODEL), lambda b: (b, 0, 0)),
            _const_spec((1, D_MODEL)), _const_spec((D_MODEL, 2 * BRANCH_W)),
        ],
        out_specs=pl.BlockSpec((1, mlen, 2 * BRANCH_W), lambda b: (b, 0, 0)),
        out_shape=jax.ShapeDtypeStruct((bsz, mlen, 2 * BRANCH_W), BF16),
        compiler_params=_params(1),
        name="memkv",
    )(mem, g_mem, w_kv)


def _merge_kernel(x_ref, gpre_ref, gpost_ref, ya_ref, yb_ref, yc_ref, bz_ref, cz_ref, mq_ref, mz_ref, mkv_ref,
                  wg_ref, wbr_ref, wo_ref, o_ref):
    x = x_ref[...]
    h = _rms(x, gpre_ref[...]).astype(BF16)

    mq = mq_ref[...]
    scale = M_HD ** -0.5
    ym = []
    for hd in range(M_HEADS):
        mk = mkv_ref[0, :, hd * M_HD:(hd + 1) * M_HD]
        mv = mkv_ref[0, :, BRANCH_W + hd * M_HD:BRANCH_W + (hd + 1) * M_HD]
        s = _dot_nt(mq[:, hd * M_HD:(hd + 1) * M_HD], mk) * scale
        e = jnp.exp(s - jnp.max(s, axis=-1, keepdims=True))
        p = e / jnp.sum(e, axis=-1, keepdims=True)
        ym.append(_dot(p.astype(BF16), mv))
    y_m = jnp.concatenate(ym, axis=-1)

    ys = (
        ya_ref[...],
        (yb_ref[...].astype(F32) * _silu(bz_ref[...].astype(F32))).astype(BF16),
        (yc_ref[...].astype(F32) * _silu(cz_ref[...].astype(F32))).astype(BF16),
        (y_m * _silu(mz_ref[...].astype(F32))).astype(BF16),
    )
    merged = None
    for nb in range(N_BRANCH):
        gate = jax.nn.sigmoid(_dot(h, wg_ref[:, nb * D_MODEL:(nb + 1) * D_MODEL]))
        term = gate * _dot(ys[nb], wbr_ref[nb])
        merged = term if merged is None else merged + term
    out = _dot(merged.astype(BF16), wo_ref[...])
    o_ref[...] = x + _rms(out, gpost_ref[...])


def _merge(x2, g_pre, g_post, y_a, y_b, y_c, p_main, mkv, w_gate, w_br, w_o, seq, tz):
    n = x2.shape[0]
    nq = seq // tz
    mlen = mkv.shape[1]
    act = lambda c: pl.BlockSpec((tz, BRANCH_W), lambda i, c=c: (i, c))
    return pl.pallas_call(
        _merge_kernel,
        grid=(n // tz,),
        in_specs=[
            pl.BlockSpec((tz, D_MODEL), lambda i: (i, 0)),
            _const_spec((1, D_MODEL)), _const_spec((1, D_MODEL)),
            act(0), act(0), act(0),
            act(COL_B_Z // 4), act(COL_C_Z // 4), act(COL_M_Q // 4), act(COL_M_Z // 4),
            pl.BlockSpec((1, mlen, 2 * BRANCH_W), lambda i: (i // nq, 0, 0)),
            _const_spec((D_MODEL, N_BRANCH * D_MODEL)),
            _const_spec((N_BRANCH, BRANCH_W, D_MODEL)),
            _const_spec((D_MODEL, D_MODEL)),
        ],
        out_specs=pl.BlockSpec((tz, D_MODEL), lambda i: (i, 0)),
        out_shape=jax.ShapeDtypeStruct((n, D_MODEL), F32),
        compiler_params=_params(1),
        name="merge",
    )(x2, g_pre, g_post, y_a, y_b, y_c, p_main, p_main, p_main, p_main, mkv, w_gate, w_br, w_o)


def _pack_w_in(w):
    (a_u, a_v, a_z, b_q, b_kc, b_vc, b_ks, b_vs, b_kw, b_vw, b_g, b_z,
     c_q, c_k, c_v, c_f, c_z, m_q, m_z) = jnp.split(w, IN_SPLITS, axis=1)
    dup = lambda t: jnp.concatenate([t[:, :B_HD], t[:, :B_HD], t[:, B_HD:], t[:, B_HD:]], axis=1)
    main = jnp.concatenate([a_u, a_v, a_z, b_q, b_z, c_q, c_k, c_v, c_z, m_q, m_z,
                            dup(b_ks), dup(b_kw), dup(b_vs), dup(b_vw), b_kc, b_vc], axis=1)
    small = jnp.concatenate([b_g, c_f, jnp.zeros((w.shape[0], LANE - b_g.shape[1] - c_f.shape[1]), w.dtype)], axis=1)
    return main.astype(BF16), small.astype(BF16)


def _pack_compress(pos, w1, w2):
    eye = jnp.eye(B_KV, dtype=F32)
    w1r = w1.reshape(2, CMP_STRIDE, B_HD, B_HD)
    w1big = jnp.einsum('hlde,gk->hlgdke', w1r, eye).reshape(2, CMP_STRIDE * B_KV * B_HD, B_KV * B_HD)
    w2big = jnp.einsum('de,gk,u->gdkue', w2, eye, jnp.ones((2,), F32)).reshape(B_KV * B_HD, 2 * B_KV * B_HD)
    pos2 = jnp.broadcast_to(pos.reshape(2, CMP_STRIDE, 1, B_HD), (2, CMP_STRIDE, B_KV, B_HD)).reshape(2, -1)
    return pos2.astype(F32), w1big.astype(BF16), w2big.astype(BF16)


def _rope_tables(seq):
    half = B_HD // 2
    freqs = ROPE_THETA ** (-jnp.arange(half, dtype=F32) / half)
    ang = jnp.arange(seq, dtype=F32)[:, None] * freqs[None, :]
    cos, sin = jnp.cos(ang), jnp.sin(ang)
    cos_t = jnp.concatenate([cos, cos, cos, cos], axis=1)
    sin_t = jnp.concatenate([-sin, sin, -sin, sin], axis=1)
    return dict(cos=cos_t, sin=sin_t)


def _causal_bias(tq, nlane):
    k = np.arange(tq)[:, None]
    t = (np.arange(nlane) & (tq - 1))[None, :]
    return jnp.asarray(np.where(k <= t, 0.0, NEG), F32)


def _nsa_constants(seq, tq):
    nc = (seq - CMP_LEN) // CMP_STRIDE + 1
    nsel = seq // SEL_LEN
    nselp = -(-nsel // 8) * 8
    ncp = seq // CMP_STRIDE
    ci = np.arange(ncp) * CMP_STRIDE
    sj = np.arange(nselp) * SEL_LEN
    ovt = ((ci[None, :] <= sj[:, None] + SEL_LEN - 1) & (ci[None, :] + CMP_LEN - 1 >= sj[:, None])
           & (np.arange(ncp)[None, :] < nc) & (np.arange(nselp)[:, None] < nsel))
    nlane = B_GQ * tq
    r = np.arange(2 * WIN + tq)[:, None]
    t = (np.arange(nlane) & (tq - 1))[None, :]
    wb = np.where((r > t) & (r <= t + WIN), 0.0, NEG)
    return dict(ovt=jnp.asarray(ovt, BF16), nsa_cb=_causal_bias(tq, nlane), nsa_wb=jnp.asarray(wb, F32))


def _fcum_constants(seq):
    ntile = seq // LANE
    nrow = C_HEADS * ntile
    umat = np.arange(LANE)[:, None] <= np.arange(LANE)[None, :]
    r = np.arange(nrow)
    mmat = (r[:, None] // ntile == r[None, :] // ntile) & (r[None, :] < r[:, None])
    return dict(umat=jnp.asarray(umat, BF16), mmat=jnp.asarray(mmat, BF16))


def _layer(x2, mem, bsz, seq, tb, w_in, g_pre, g_post, g_mem, w_mem_kv, a_ln_g, a_ln_b, a_ws, a_bs,
           pos_k, w1_k, w2_k, pos_v, w1_v, w2_v, c_fbias, w_br, w_gate, w_o):
    w_main, w_small = _pack_w_in(w_in)
    p_main, p_small = _proj(x2, g_pre.reshape(1, -1), w_main, w_small, tb['cos'], tb['sin'], seq, tm=min(512, seq))

    y_a = _gmlp(p_main, a_ln_g.reshape(1, -1), a_ln_b.reshape(1, -1), a_ws, a_bs.T, ta=min(512, seq))

    to16 = lambda col: p_main[:, col * LANE:(col + 1) * LANE].reshape(bsz, seq // CMP_STRIDE, CMP_STRIDE * LANE)
    cmp_k = _compress(to16(COL_KC), *_pack_compress(pos_k, w1_k, w2_k))
    cmp_v = _compress(to16(COL_VC), *_pack_compress(pos_v, w1_v, w2_v))
    glt = p_small[:, SMALL_G0:SMALL_G0 + 3 * B_HEADS].reshape(bsz, seq, B_KV, 3 * B_GQ).transpose(0, 2, 3, 1)
    y_b = _nsa(p_main, cmp_k, cmp_v, glt, tb['ovt'], tb['nsa_cb'], tb['nsa_wb'], bsz, seq, tq=NSA_TQ)

    ntile = seq // LANE
    f_tiles = (p_small[:, SMALL_F0:SMALL_F0 + C_HEADS].reshape(bsz, seq, C_HEADS).transpose(0, 2, 1)
               .reshape(bsz, C_HEADS * ntile, LANE))
    bias_col = jnp.repeat(c_fbias.astype(F32), ntile).reshape(-1, 1)
    cum, c_hi, c_mid, c_lo = _fcum(f_tiles, bias_col, tb['umat'], tb['mmat'])
    parts = jnp.stack([c_hi, c_mid, c_lo], axis=-1).reshape(bsz, C_HEADS // 2, 2, seq, 3)
    parts = parts.transpose(0, 1, 3, 2, 4).reshape(bsz, C_HEADS // 2, seq, 6)
    k_aug = jnp.concatenate([jnp.ones((bsz, C_HEADS // 2, seq, 3), BF16), parts,
                             jnp.zeros((bsz, C_HEADS // 2, seq, LANE - 9), BF16)], axis=-1)
    y_c = _fox(p_main, k_aug, cum.reshape(bsz, C_HEADS, seq, 1), tb['fox_cb'], bsz, seq, tq=FOX_TQ)

    mkv = _memkv(mem, g_mem.reshape(1, -1), w_mem_kv.astype(BF16))
    wg = w_gate.reshape(D_MODEL, N_BRANCH * D_MODEL).astype(BF16)
    return _merge(x2, g_pre.reshape(1, -1), g_post.reshape(1, -1), y_a, y_b, y_c, p_main, mkv,
                  wg, w_br.astype(BF16), w_o.astype(BF16), seq, tz=min(256, seq))


def kernel(x, mem, w_in, g_pre, g_post, g_mem, w_mem_kv, a_ln_g, a_ln_b, a_ws, a_bs, b_cmp_pos_k, b_cmp_w1_k,
           b_cmp_w2_k, b_cmp_pos_v, b_cmp_w1_v, b_cmp_w2_v, c_fbias, w_br, w_gate, w_o):
    bsz, seq, d = x.shape
    tables = dict(**_rope_tables(seq), **_nsa_constants(seq, NSA_TQ), **_fcum_constants(seq),
                  fox_cb=_causal_bias(FOX_TQ, 2 * FOX_TQ))
    x2 = x.reshape(bsz * seq, d)
    for l in range(w_in.shape[0]):
        x2 = _layer(x2, mem, bsz, seq, tables, w_in[l], g_pre[l], g_post[l], g_mem[l], w_mem_kv[l],
                    a_ln_g[l], a_ln_b[l], a_ws[l], a_bs[l], b_cmp_pos_k[l], b_cmp_w1_k[l], b_cmp_w2_k[l],
                    b_cmp_pos_v[l], b_cmp_w1_v[l], b_cmp_w2_v[l], c_fbias[l], w_br[l], w_gate[l], w_o[l])
    return x2.reshape(bsz, seq, d)
```

```python
import functools

import numpy as np
import jax
import jax.numpy as jnp
from jax import lax
from jax.experimental import pallas as pl
from jax.experimental.pallas import tpu as pltpu

F32 = jnp.float32
BF16 = jnp.bfloat16

D_MODEL = 1024
N_BRANCH = 4
BRANCH_W = 512
A_GROUPS = 4
A_CHUNK = 128
B_HEADS = 8
B_KV = 2
B_GQ = B_HEADS // B_KV
B_HD = 64
CMP_LEN = 32
CMP_STRIDE = 16
SEL_LEN = 64
SEL_N = 8
WIN = 256
C_HEADS = 8
C_HD = 64
M_HEADS = 4
M_HD = BRANCH_W // M_HEADS
ROPE_THETA = 10000.0
EPS = 1e-6
NEG = -1e30
BIG = 1e9

IN_SIZES = (
    BRANCH_W, BRANCH_W, BRANCH_W,
    B_HEADS * B_HD, B_KV * B_HD, B_KV * B_HD, B_KV * B_HD,
    B_KV * B_HD, B_KV * B_HD, B_KV * B_HD, B_HEADS * 3, BRANCH_W,
    C_HEADS * C_HD, C_HEADS * C_HD, C_HEADS * C_HD, C_HEADS, BRANCH_W,
    M_HEADS * M_HD, BRANCH_W,
)
IN_SPLITS = tuple(int(c) for c in np.cumsum(IN_SIZES)[:-1])

LANE = 128
VMEM_LIMIT = 56 * 1024 * 1024

NSA_TQ = 256
FOX_TQ = 256
FOX_PAIRS = 4

COL_A_U, COL_A_V, COL_A_Z = 0, 4, 8
COL_B_Q, COL_B_Z = 12, 16
COL_C_Q, COL_C_K, COL_C_V, COL_C_Z = 20, 24, 28, 32
COL_M_Q, COL_M_Z = 36, 40
COL_KS, COL_KW, COL_VS, COL_VW = 44, 46, 48, 50
COL_KC, COL_VC = 52, 53
N_COLS = 54
ROPE_COLS = frozenset(list(range(COL_B_Q, COL_B_Q + 4)) + [COL_KS, COL_KS + 1, COL_KW, COL_KW + 1, COL_KC])
W_MAIN = N_COLS * LANE
SMALL_G0, SMALL_F0 = 0, B_HEADS * 3

_NT = (((1,), (1,)), ((), ()))
_TN = (((0,), (0,)), ((), ()))


def _dot(a, b):
    return jnp.dot(a, b, preferred_element_type=F32)


def _dot_nt(a, b):
    return lax.dot_general(a, b, _NT, preferred_element_type=F32)


def _dot_tn(a, b):
    return lax.dot_general(a, b, _TN, preferred_element_type=F32)


def _online_steps(sts, vs, m_sc, l_sc, acc_sc):
    pes, alphas = [], []
    for g, st in enumerate(sts):
        m_old = m_sc[g]
        m_new = jnp.maximum(m_old, jnp.max(st, axis=0, keepdims=True))
        alpha = jnp.exp(m_old - m_new)
        pe = jnp.exp(st - m_new)
        l_sc[g] = alpha * l_sc[g] + jnp.sum(pe, axis=0, keepdims=True)
        m_sc[g] = m_new
        pes.append(pe.astype(BF16))
        alphas.append(alpha)
    for g, (pe, alpha) in enumerate(zip(pes, alphas)):
        for part, (v, cols) in enumerate(vs[g]):
            acc_sc[g, part] = alpha[:, cols] * acc_sc[g, part] + _dot_tn(v, pe[:, cols])


def _split3(a):
    hi = a.astype(BF16)
    r1 = a - hi.astype(F32)
    mid = r1.astype(BF16)
    lo = (r1 - mid.astype(F32)).astype(BF16)
    return hi, mid, lo


def _rms(x, g):
    return x * lax.rsqrt(jnp.mean(x * x, axis=-1, keepdims=True) + EPS) * g


def _silu(x):
    return x * jax.nn.sigmoid(x)


def _const_spec(shape):
    nd = len(shape)
    return pl.BlockSpec(shape, lambda *_: (0,) * nd, pipeline_mode=pl.Buffered(1))


def _params(n_grid):
    return pltpu.CompilerParams(dimension_semantics=("arbitrary",) * n_grid, vmem_limit_bytes=VMEM_LIMIT)


def _proj_kernel(x_ref, g_ref, w_ref, ws_ref, cos_ref, sin_ref, o_ref, os_ref):
    h = _rms(x_ref[...], g_ref[...]).astype(BF16)
    tm = h.shape[0]
    lane = lax.broadcasted_iota(jnp.int32, (tm, LANE), 1)
    first_half = (lane & (B_HD - 1)) < (B_HD // 2)
    cos = cos_ref[...]
    sin = sin_ref[...]
    chunk = 2
    for c0 in range(0, N_COLS, chunk):
        acc = _dot(h, w_ref[:, c0 * LANE:(c0 + chunk) * LANE])
        for j in range(chunk):
            a = acc[:, j * LANE:(j + 1) * LANE]
            if c0 + j in ROPE_COLS:
                swapped = jnp.where(first_half, pltpu.roll(a, LANE - B_HD // 2, 1), pltpu.roll(a, B_HD // 2, 1))
                a = a * cos + swapped * sin
            o_ref[:, (c0 + j) * LANE:(c0 + j + 1) * LANE] = a.astype(BF16)
    os_ref[...] = _dot(h, ws_ref[...])


def _proj(x2, g_pre, w_main, w_small, cos_t, sin_t, seq, tm):
    n = x2.shape[0]
    nq = seq // tm
    return pl.pallas_call(
        _proj_kernel,
        grid=(n // tm,),
        in_specs=[
            pl.BlockSpec((tm, D_MODEL), lambda i: (i, 0)),
            _const_spec((1, D_MODEL)),
            _const_spec((D_MODEL, W_MAIN)),
            _const_spec((D_MODEL, LANE)),
            pl.BlockSpec((tm, LANE), lambda i: (i % nq, 0)),
            pl.BlockSpec((tm, LANE), lambda i: (i % nq, 0)),
        ],
        out_specs=[
            pl.BlockSpec((tm, W_MAIN), lambda i: (i, 0)),
            pl.BlockSpec((tm, LANE), lambda i: (i, 0)),
        ],
        out_shape=[jax.ShapeDtypeStruct((n, W_MAIN), BF16), jax.ShapeDtypeStruct((n, LANE), F32)],
        compiler_params=_params(1),
        name="proj",
    )(x2, g_pre, w_main, w_small, cos_t, sin_t)


def _gmlp_kernel(u_ref, v_ref, z_ref, lng_ref, lnb_ref, ws_ref, bs_ref, o_ref):
    ta = u_ref.shape[0]
    u = jax.nn.gelu(u_ref[...].astype(F32))
    v = jax.nn.gelu(v_ref[...].astype(F32))
    mu = jnp.mean(v, axis=-1, keepdims=True)
    vc = v - mu
    v = vc * lax.rsqrt(jnp.mean(vc * vc, axis=-1, keepdims=True) + EPS) * lng_ref[...] + lnb_ref[...]
    vb = v.astype(BF16)
    gate = _silu(z_ref[...].astype(F32))
    row = lax.broadcasted_iota(jnp.int32, (A_CHUNK, A_CHUNK), 0)
    col = lax.broadcasted_iota(jnp.int32, (A_CHUNK, A_CHUNK), 1)
    tri = col <= row
    for g in range(A_GROUPS):
        w = jnp.where(tri, ws_ref[g], 0.0).astype(BF16)
        b = bs_ref[:, g:g + 1]
        for c in range(ta // A_CHUNK):
            rs = slice(c * A_CHUNK, (c + 1) * A_CHUNK)
            gs = slice(g * LANE, (g + 1) * LANE)
            s = _dot(w, vb[rs, gs]) + b
            o_ref[rs, gs] = (u[rs, gs] * s * gate[rs, gs]).astype(BF16)


def _gmlp(p_main, ln_g, ln_b, ws, bs_t, ta):
    n = p_main.shape[0]
    blk = lambda c: pl.BlockSpec((ta, BRANCH_W), lambda i, c=c: (i, c))
    return pl.pallas_call(
        _gmlp_kernel,
        grid=(n // ta,),
        in_specs=[
            blk(COL_A_U // 4), blk(COL_A_V // 4), blk(COL_A_Z // 4),
            _const_spec((1, BRANCH_W)), _const_spec((1, BRANCH_W)),
            _const_spec((A_GROUPS, A_CHUNK, A_CHUNK)), _const_spec((A_CHUNK, A_GROUPS)),
        ],
        out_specs=pl.BlockSpec((ta, BRANCH_W), lambda i: (i, 0)),
        out_shape=jax.ShapeDtypeStruct((n, BRANCH_W), BF16),
        compiler_params=_params(1),
        name="gmlp",
    )(p_main, p_main, p_main, ln_g, ln_b, ws, bs_t)


def _compress_kernel(x_ref, pos_ref, w1_ref, w2_ref, o_ref):
    x = x_ref[0].astype(F32)
    top = _dot((x + pos_ref[0:1, :]).astype(BF16), w1_ref[0])
    bot = _dot((x + pos_ref[1:2, :]).astype(BF16), w1_ref[1])
    nrow = x.shape[0]
    a = top + pltpu.roll(bot, nrow - 1, 0)
    o_ref[0] = _dot(_silu(a).astype(BF16), w2_ref[...]).astype(BF16)


def _compress(x16, pos2, w1big, w2big):
    bsz, nrow, width = x16.shape
    return pl.pallas_call(
        _compress_kernel,
        grid=(bsz,),
        in_specs=[
            pl.BlockSpec((1, nrow, width), lambda b: (b, 0, 0)),
            _const_spec((2, width)),
            _const_spec((2, width, LANE)),
            _const_spec((LANE, 2 * LANE)),
        ],
        out_specs=pl.BlockSpec((1, nrow, 2 * LANE), lambda b: (b, 0, 0)),
        out_shape=jax.ShapeDtypeStruct((bsz, nrow, 2 * LANE), BF16),
        compiler_params=_params(1),
        name="compress",
    )(x16, pos2, w1big, w2big)


def _softmax_t(st, mask):
    sm = jnp.where(mask, st, NEG)
    m = jnp.max(sm, axis=0, keepdims=True)
    e = jnp.where(mask, jnp.exp(sm - m), 0.0)
    return e, jnp.sum(e, axis=0, keepdims=True)


def _nsa_kernel(q_ref, ks_ref, vs_ref, kw_ref, vw_ref, kcmp_ref, vcmp_ref, glt_ref, ovt_ref, cb_ref, wb_ref, o_ref,
                bias_sc, m_sc, l_sc, acc_sc, part_sc, *, tq, seq):
    i = pl.program_id(1)
    qs = i * tq
    kc = tq
    scale = B_HD ** -0.5
    nsel = seq // SEL_LEN
    nc = (seq - CMP_LEN) // CMP_STRIDE + 1
    nlane = B_GQ * tq
    lane = lax.broadcasted_iota(jnp.int32, (tq, LANE), 1)
    lo = lane < B_HD
    tpos = qs + (lax.broadcasted_iota(jnp.int32, (1, nlane), 1) & (tq - 1))
    ovt = ovt_ref[...]
    nselp = ovt.shape[0]
    j = lax.broadcasted_iota(jnp.int32, (nselp, tq), 0)
    cur = (qs + lax.broadcasted_iota(jnp.int32, (nselp, tq), 1)) // SEL_LEN
    forced = (j == 0) | (j == cur) | (j == cur - 1)
    future = j > cur

    qg, o_cmp = [], []
    for g in range(B_KV):
        qa = q_ref[:, 2 * g * LANE:(2 * g + 1) * LANE].astype(F32) * scale
        qb = q_ref[:, (2 * g + 1) * LANE:(2 * g + 2) * LANE].astype(F32) * scale
        q = jnp.concatenate([jnp.where(lo, qa, 0.0), jnp.where(lo, 0.0, qa),
                             jnp.where(lo, qb, 0.0), jnp.where(lo, 0.0, qb)], axis=0)
        q = q.T.astype(BF16)
        qg.append(q)

        st = _dot(kcmp_ref[0, :, g * LANE:(g + 1) * LANE], q)
        cpos = lax.broadcasted_iota(jnp.int32, st.shape, 0)
        ok = ((cpos * CMP_STRIDE + (CMP_LEN - 1)) <= tpos) & (cpos < nc)
        e, l = _softmax_t(st, ok)
        p = e / jnp.where(l > 0.0, l, 1.0)
        o_cmp.append(_dot_tn(vcmp_ref[0, :, g * LANE:g * LANE + B_HD], p.astype(BF16)))

        psum = p[:, 0:tq] + p[:, tq:2 * tq] + p[:, 2 * tq:3 * tq] + p[:, 3 * tq:4 * tq]
        hi = psum.astype(BF16)
        lo_part = (psum - hi.astype(F32)).astype(BF16)
        imp = _dot(ovt, hi) + _dot(ovt, lo_part)
        imp = jnp.where(forced, BIG, jnp.where(future, NEG, imp))
        imp = jnp.where(j < nsel, imp, -jnp.inf)
        rank = jnp.zeros((nselp, tq), F32)
        for ii in range(nsel):
            r = imp[ii:ii + 1, :]
            beats = (r > imp) | ((r == imp) & (ii < j))
            rank = rank + jnp.where(beats, 1.0, 0.0)
        bias_t = jnp.where((rank < float(min(SEL_N, nsel))) & (j < nsel), 0.0, NEG)
        bias_sc[g] = jnp.concatenate([bias_t] * B_GQ, axis=1)

    m_sc[...] = jnp.full(m_sc.shape, NEG, F32)
    l_sc[...] = jnp.zeros(l_sc.shape, F32)
    acc_sc[...] = jnp.zeros(acc_sc.shape, F32)

    def step(k0, diagonal):
        sts = [_dot(ks_ref[pl.ds(k0, kc), g * LANE:(g + 1) * LANE], qg[g]) for g in range(B_KV)]
        j0 = k0 // SEL_LEN
        for g in range(B_KV):
            sts[g] = sts[g] + jnp.concatenate(
                [jnp.broadcast_to(bias_sc[g, pl.ds(j0 + jb, 1), :], (SEL_LEN, nlane))
                 for jb in range(kc // SEL_LEN)], axis=0)
            if diagonal:
                sts[g] = sts[g] + cb_ref[...]
        vs = [[(vs_ref[pl.ds(k0, kc), g * LANE:g * LANE + B_HD], slice(None))] for g in range(B_KV)]
        _online_steps(sts, vs, m_sc, l_sc, acc_sc)

    step(pl.multiple_of(qs, tq), True)

    wlen = min(WIN + tq, seq)
    w0 = pl.multiple_of(jnp.clip(qs - WIN, 0, seq - wlen), tq)
    wb = wb_ref[pl.ds(pl.multiple_of(WIN - (qs - w0), tq), wlen), :]
    g_all = jax.nn.sigmoid(glt_ref[0])
    gate = lambda g, r, n: g_all[g, 3 * r + n:3 * r + n + 1, :]
    for g in range(B_KV):
        st = _dot(kw_ref[pl.ds(w0, wlen), g * LANE:(g + 1) * LANE], qg[g]) + wb
        e = jnp.exp(st - jnp.max(st, axis=0, keepdims=True))
        o_win = _dot_tn(vw_ref[pl.ds(w0, wlen), g * LANE:g * LANE + B_HD], e.astype(BF16))
        o_win = o_win / jnp.sum(e, axis=0, keepdims=True)
        for r in range(B_GQ):
            cs = slice(r * tq, (r + 1) * tq)
            part_sc[g, :, cs] = gate(g, r, 0) * o_cmp[g][:, cs] + gate(g, r, 2) * o_win[:, cs]

    def body(c, carry):
        step(pl.multiple_of(c * kc, kc), False)
        return carry

    lax.fori_loop(0, i, body, 0)

    for g in range(B_KV):
        o_sel = acc_sc[g, 0] / l_sc[g]
        outs = []
        for r in range(B_GQ):
            cs = slice(r * tq, (r + 1) * tq)
            outs.append(part_sc[g, :, cs] + gate(g, r, 1) * o_sel[:, cs])
        o_ref[:, 2 * g * LANE:(2 * g + 1) * LANE] = jnp.concatenate(outs[0:2], axis=0).T.astype(BF16)
        o_ref[:, (2 * g + 1) * LANE:(2 * g + 2) * LANE] = jnp.concatenate(outs[2:4], axis=0).T.astype(BF16)


def _nsa(p_main, cmp_k, cmp_v, gate_logits_t, ovt, causal_bias, window_bias, bsz, seq, tq):
    n = p_main.shape[0]
    nq = seq // tq
    kv_spec = lambda col: pl.BlockSpec((seq, B_KV * LANE), lambda b, i, col=col: (b, col // B_KV))
    nrow = cmp_k.shape[1]
    cmp_spec = pl.BlockSpec((1, nrow, B_KV * LANE), lambda b, i: (b, 0, 0))
    nlane = B_GQ * tq
    return pl.pallas_call(
        functools.partial(_nsa_kernel, tq=tq, seq=seq),
        grid=(bsz, nq),
        in_specs=[
            pl.BlockSpec((tq, BRANCH_W), lambda b, i: (b * nq + i, COL_B_Q // 4)),
            kv_spec(COL_KS), kv_spec(COL_VS), kv_spec(COL_KW), kv_spec(COL_VW),
            cmp_spec, cmp_spec,
            pl.BlockSpec((1, B_KV, 3 * B_GQ, tq), lambda b, i: (b, 0, 0, i)),
            _const_spec(ovt.shape), _const_spec(causal_bias.shape), _const_spec(window_bias.shape),
        ],
        out_specs=pl.BlockSpec((tq, BRANCH_W), lambda b, i: (b * nq + i, 0)),
        out_shape=jax.ShapeDtypeStruct((n, BRANCH_W), BF16),
        scratch_shapes=[pltpu.VMEM((B_KV, ovt.shape[0], nlane), F32), pltpu.VMEM((B_KV, 1, nlane), F32),
                        pltpu.VMEM((B_KV, 1, nlane), F32), pltpu.VMEM((B_KV, 1, B_HD, nlane), F32),
                        pltpu.VMEM((B_KV, B_HD, nlane), F32)],
        compiler_params=_params(2),
        name="nsa",
    )(p_main, p_main, p_main, p_main, p_main, cmp_k, cmp_v, gate_logits_t, ovt, causal_bias, window_bias)


def _fcum_kernel(f_ref, b_ref, u_ref, m_ref, o_ref, hi_ref, mid_ref, lo_ref):
    x = f_ref[0] + b_ref[...]
    lf = jnp.minimum(x, 0.0) - jnp.log(1.0 + jnp.exp(-jnp.abs(x)))
    u = u_ref[...]
    within = sum(_dot(t, u) for t in _split3(lf))
    mm = m_ref[...]
    before = sum(_dot(mm, t) for t in _split3(within))
    c = within + before[:, LANE - 1:LANE]
    o_ref[0] = c
    hi_ref[0], mid_ref[0], lo_ref[0] = _split3(c)


def _fcum(f_tiles, bias_col, umat, mmat):
    bsz, nrow, _ = f_tiles.shape
    spec = pl.BlockSpec((1, nrow, LANE), lambda b: (b, 0, 0))
    return pl.pallas_call(
        _fcum_kernel,
        grid=(bsz,),
        in_specs=[spec, _const_spec((nrow, 1)), _const_spec((LANE, LANE)), _const_spec((nrow, nrow))],
        out_specs=[spec] * 4,
        out_shape=[jax.ShapeDtypeStruct((bsz, nrow, LANE), F32)] + [jax.ShapeDtypeStruct((bsz, nrow, LANE), BF16)] * 3,
        compiler_params=_params(1),
        name="fcum",
    )(f_tiles, bias_col, umat, mmat)


def _fox_kernel(q_ref, k_ref, v_ref, kaug_ref, ccol_ref, cb_ref, o_ref, m_sc, l_sc, acc_sc, *, tq):
    i = pl.program_id(2)
    kc = tq
    scale = C_HD ** -0.5
    lane = lax.broadcasted_iota(jnp.int32, (tq, LANE), 1)
    lo = lane < C_HD
    qxs = []
    for pr in range(FOX_PAIRS):
        q2 = q_ref[:, pr * LANE:(pr + 1) * LANE].astype(F32) * scale
        halves = []
        for hd in range(2):
            hi, mid, lw = _split3(ccol_ref[0, 2 * pr + hd])
            aug = jnp.where(lane == 0, hi.astype(F32), jnp.where(lane == 1, mid.astype(F32), jnp.where(
                lane == 2, lw.astype(F32), jnp.where((lane >= 3 + 3 * hd) & (lane < 6 + 3 * hd), -1.0, 0.0))))
            qh = jnp.where(lo, q2, 0.0) if hd == 0 else jnp.where(lo, 0.0, q2)
            halves.append(jnp.concatenate([qh, aug], axis=1))
        qxs.append(jnp.concatenate(halves, axis=0).T.astype(BF16))

    m_sc[...] = jnp.full(m_sc.shape, NEG, F32)
    l_sc[...] = jnp.zeros(l_sc.shape, F32)
    acc_sc[...] = jnp.zeros(acc_sc.shape, F32)

    def scores(c):
        k0 = pl.multiple_of(c * kc, kc)
        sts = []
        for pr in range(FOX_PAIRS):
            cols = slice(pr * LANE, (pr + 1) * LANE)
            kx = jnp.concatenate([k_ref[pl.ds(k0, kc), cols], kaug_ref[0, pr, pl.ds(k0, kc), :]], axis=1)
            sts.append(_dot(kx, qxs[pr]))
        return tuple(sts)

    def consume(c, sts):
        k0 = pl.multiple_of(c * kc, kc)
        vs = [[(v_ref[pl.ds(k0, kc), pr * LANE + hd * C_HD:pr * LANE + (hd + 1) * C_HD],
                slice(hd * tq, (hd + 1) * tq)) for hd in range(2)] for pr in range(FOX_PAIRS)]
        _online_steps(sts, vs, m_sc, l_sc, acc_sc)

    def body(c, carry):
        consume(c, scores(c))
        return carry

    lax.fori_loop(0, i, body, 0)
    consume(i, [st + cb_ref[...] for st in scores(i)])

    for pr in range(FOX_PAIRS):
        l = l_sc[pr]
        ot = jnp.concatenate([acc_sc[pr, 0] / l[:, 0:tq], acc_sc[pr, 1] / l[:, tq:2 * tq]], axis=0)
        o_ref[:, pr * LANE:(pr + 1) * LANE] = ot.T.astype(BF16)


def _fox(p_main, k_aug, c_col, causal_bias, bsz, seq, tq):
    n = p_main.shape[0]
    nq = seq // tq
    ngrp = C_HEADS // 2 // FOX_PAIRS
    wide = FOX_PAIRS * LANE
    return pl.pallas_call(
        functools.partial(_fox_kernel, tq=tq),
        grid=(bsz, ngrp, nq),
        in_specs=[
            pl.BlockSpec((tq, wide), lambda b, h, i: (b * nq + i, COL_C_Q // FOX_PAIRS + h)),
            pl.BlockSpec((seq, wide), lambda b, h, i: (b, COL_C_K // FOX_PAIRS + h)),
            pl.BlockSpec((seq, wide), lambda b, h, i: (b, COL_C_V // FOX_PAIRS + h)),
            pl.BlockSpec((1, FOX_PAIRS, seq, LANE), lambda b, h, i: (b, h, 0, 0)),
            pl.BlockSpec((1, 2 * FOX_PAIRS, tq, 1), lambda b, h, i: (b, h, i, 0)),
            _const_spec(causal_bias.shape),
        ],
        out_specs=pl.BlockSpec((tq, wide), lambda b, h, i: (b * nq + i, h)),
        out_shape=jax.ShapeDtypeStruct((n, BRANCH_W), BF16),
        scratch_shapes=[pltpu.VMEM((FOX_PAIRS, 1, 2 * tq), F32), pltpu.VMEM((FOX_PAIRS, 1, 2 * tq), F32),
                        pltpu.VMEM((FOX_PAIRS, 2, C_HD, tq), F32)],
        compiler_params=_params(3),
        name="fox",
    )(p_main, p_main, p_main, k_aug, c_col, causal_bias)


def _memkv_kernel(mem_ref, g_ref, w_ref, o_ref):
    h = _rms(mem_ref[0], g_ref[...]).astype(BF16)
    o_ref[0] = _dot(h, w_ref[...]).astype(BF16)


def _memkv(mem, g_mem, w_kv):
    bsz, mlen, _ = mem.shape
    return pl.pallas_call(
        _memkv_kernel,
        grid=(bsz,),
        in_specs=[
            pl.BlockSpec((1, mlen, D_MODEL), lambda b: (b, 0, 0)),
            _const_spec((1, D_MODEL)), _const_spec((D_MODEL, 2 * BRANCH_W)),
        ],
        out_specs=pl.BlockSpec((1, mlen, 2 * BRANCH_W), lambda b: (b, 0, 0)),
        out_shape=jax.ShapeDtypeStruct((bsz, mlen, 2 * BRANCH_W), BF16),
        compiler_params=_params(1),
        name="memkv",
    )(mem, g_mem, w_kv)


def _merge_kernel(x_ref, gpre_ref, gpost_ref, ya_ref, yb_ref, yc_ref, bz_ref, cz_ref, mq_ref, mz_ref, mkv_ref,
                  wg_ref, wbr_ref, wo_ref, o_ref):
    x = x_ref[...]
    h = _rms(x, gpre_ref[...]).astype(BF16)

    mq = mq_ref[...]
    scale = M_HD ** -0.5
    ym = []
    for hd in range(M_HEADS):
        mk = mkv_ref[0, :, hd * M_HD:(hd + 1) * M_HD]
        mv = mkv_ref[0, :, BRANCH_W + hd * M_HD:BRANCH_W + (hd + 1) * M_HD]
        s = _dot_nt(mq[:, hd * M_HD:(hd + 1) * M_HD], mk) * scale
        e = jnp.exp(s - jnp.max(s, axis=-1, keepdims=True))
        p = e / jnp.sum(e, axis=-1, keepdims=True)
        ym.append(_dot(p.astype(BF16), mv))
    y_m = jnp.concatenate(ym, axis=-1)

    ys = (
        ya_ref[...],
        (yb_ref[...].astype(F32) * _silu(bz_ref[...].astype(F32))).astype(BF16),
        (yc_ref[...].astype(F32) * _silu(cz_ref[...].astype(F32))).astype(BF16),
        (y_m * _silu(mz_ref[...].astype(F32))).astype(BF16),
    )
    merged = None
    for nb in range(N_BRANCH):
        gate = jax.nn.sigmoid(_dot(h, wg_ref[:, nb * D_MODEL:(nb + 1) * D_MODEL]))
        term = gate * _dot(ys[nb], wbr_ref[nb])
        merged = term if merged is None else merged + term
    out = _dot(merged.astype(BF16), wo_ref[...])
    o_ref[...] = x + _rms(out, gpost_ref[...])


def _merge(x2, g_pre, g_post, y_a, y_b, y_c, p_main, mkv, w_gate, w_br, w_o, seq, tz):
    n = x2.shape[0]
    nq = seq // tz
    mlen = mkv.shape[1]
    act = lambda c: pl.BlockSpec((tz, BRANCH_W), lambda i, c=c: (i, c))
    return pl.pallas_call(
        _merge_kernel,
        grid=(n // tz,),
        in_specs=[
            pl.BlockSpec((tz, D_MODEL), lambda i: (i, 0)),
            _const_spec((1, D_MODEL)), _const_spec((1, D_MODEL)),
            act(0), act(0), act(0),
            act(COL_B_Z // 4), act(COL_C_Z // 4), act(COL_M_Q // 4), act(COL_M_Z // 4),
            pl.BlockSpec((1, mlen, 2 * BRANCH_W), lambda i: (i // nq, 0, 0)),
            _const_spec((D_MODEL, N_BRANCH * D_MODEL)),
            _const_spec((N_BRANCH, BRANCH_W, D_MODEL)),
            _const_spec((D_MODEL, D_MODEL)),
        ],
        out_specs=pl.BlockSpec((tz, D_MODEL), lambda i: (i, 0)),
        out_shape=jax.ShapeDtypeStruct((n, D_MODEL), F32),
        compiler_params=_params(1),
        name="merge",
    )(x2, g_pre, g_post, y_a, y_b, y_c, p_main, p_main, p_main, p_main, mkv, w_gate, w_br, w_o)


def _pack_w_in(w):
    (a_u, a_v, a_z, b_q, b_kc, b_vc, b_ks, b_vs, b_kw, b_vw, b_g, b_z,
     c_q, c_k, c_v, c_f, c_z, m_q, m_z) = jnp.split(w, IN_SPLITS, axis=1)
    dup = lambda t: jnp.concatenate([t[:, :B_HD], t[:, :B_HD], t[:, B_HD:], t[:, B_HD:]], axis=1)
    main = jnp.concatenate([a_u, a_v, a_z, b_q, b_z, c_q, c_k, c_v, c_z, m_q, m_z,
                            dup(b_ks), dup(b_kw), dup(b_vs), dup(b_vw), b_kc, b_vc], axis=1)
    small = jnp.concatenate([b_g, c_f, jnp.zeros((w.shape[0], LANE - b_g.shape[1] - c_f.shape[1]), w.dtype)], axis=1)
    return main.astype(BF16), small.astype(BF16)


def _pack_compress(pos, w1, w2):
    eye = jnp.eye(B_KV, dtype=F32)
    w1r = w1.reshape(2, CMP_STRIDE, B_HD, B_HD)
    w1big = jnp.einsum('hlde,gk->hlgdke', w1r, eye).reshape(2, CMP_STRIDE * B_KV * B_HD, B_KV * B_HD)
    w2big = jnp.einsum('de,gk,u->gdkue', w2, eye, jnp.ones((2,), F32)).reshape(B_KV * B_HD, 2 * B_KV * B_HD)
    pos2 = jnp.broadcast_to(pos.reshape(2, CMP_STRIDE, 1, B_HD), (2, CMP_STRIDE, B_KV, B_HD)).reshape(2, -1)
    return pos2.astype(F32), w1big.astype(BF16), w2big.astype(BF16)


def _rope_tables(seq):
    half = B_HD // 2
    freqs = ROPE_THETA ** (-jnp.arange(half, dtype=F32) / half)
    ang = jnp.arange(seq, dtype=F32)[:, None] * freqs[None, :]
    cos, sin = jnp.cos(ang), jnp.sin(ang)
    cos_t = jnp.concatenate([cos, cos, cos, cos], axis=1)
    sin_t = jnp.concatenate([-sin, sin, -sin, sin], axis=1)
    return dict(cos=cos_t, sin=sin_t)


def _causal_bias(tq, nlane):
    k = np.arange(tq)[:, None]
    t = (np.arange(nlane) & (tq - 1))[None, :]
    return jnp.asarray(np.where(k <= t, 0.0, NEG), F32)


def _nsa_constants(seq, tq):
    nc = (seq - CMP_LEN) // CMP_STRIDE + 1
    nsel = seq // SEL_LEN
    nselp = -(-nsel // 8) * 8
    ncp = seq // CMP_STRIDE
    ci = np.arange(ncp) * CMP_STRIDE
    sj = np.arange(nselp) * SEL_LEN
    ovt = ((ci[None, :] <= sj[:, None] + SEL_LEN - 1) & (ci[None, :] + CMP_LEN - 1 >= sj[:, None])
           & (np.arange(ncp)[None, :] < nc) & (np.arange(nselp)[:, None] < nsel))
    nlane = B_GQ * tq
    r = np.arange(2 * WIN + tq)[:, None]
    t = (np.arange(nlane) & (tq - 1))[None, :]
    wb = np.where((r > t) & (r <= t + WIN), 0.0, NEG)
    return dict(ovt=jnp.asarray(ovt, BF16), nsa_cb=_causal_bias(tq, nlane), nsa_wb=jnp.asarray(wb, F32))


def _fcum_constants(seq):
    ntile = seq // LANE
    nrow = C_HEADS * ntile
    umat = np.arange(LANE)[:, None] <= np.arange(LANE)[None, :]
    r = np.arange(nrow)
    mmat = (r[:, None] // ntile == r[None, :] // ntile) & (r[None, :] < r[:, None])
    return dict(umat=jnp.asarray(umat, BF16), mmat=jnp.asarray(mmat, BF16))


def _layer(x2, mem, bsz, seq, tb, w_in, g_pre, g_post, g_mem, w_mem_kv, a_ln_g, a_ln_b, a_ws, a_bs,
           pos_k, w1_k, w2_k, pos_v, w1_v, w2_v, c_fbias, w_br, w_gate, w_o):
    w_main, w_small = _pack_w_in(w_in)
    p_main, p_small = _proj(x2, g_pre.reshape(1, -1), w_main, w_small, tb['cos'], tb['sin'], seq, tm=min(512, seq))

    y_a = _gmlp(p_main, a_ln_g.reshape(1, -1), a_ln_b.reshape(1, -1), a_ws, a_bs.T, ta=min(512, seq))

    to16 = lambda col: p_main[:, col * LANE:(col + 1) * LANE].reshape(bsz, seq // CMP_STRIDE, CMP_STRIDE * LANE)
    cmp_k = _compress(to16(COL_KC), *_pack_compress(pos_k, w1_k, w2_k))
    cmp_v = _compress(to16(COL_VC), *_pack_compress(pos_v, w1_v, w2_v))
    glt = p_small[:, SMALL_G0:SMALL_G0 + 3 * B_HEADS].reshape(bsz, seq, B_KV, 3 * B_GQ).transpose(0, 2, 3, 1)
    y_b = _nsa(p_main, cmp_k, cmp_v, glt, tb['ovt'], tb['nsa_cb'], tb['nsa_wb'], bsz, seq, tq=NSA_TQ)

    ntile = seq // LANE
    f_tiles = (p_small[:, SMALL_F0:SMALL_F0 + C_HEADS].reshape(bsz, seq, C_HEADS).transpose(0, 2, 1)
               .reshape(bsz, C_HEADS * ntile, LANE))
    bias_col = jnp.repeat(c_fbias.astype(F32), ntile).reshape(-1, 1)
    cum, c_hi, c_mid, c_lo = _fcum(f_tiles, bias_col, tb['umat'], tb['mmat'])
    parts = jnp.stack([c_hi, c_mid, c_lo], axis=-1).reshape(bsz, C_HEADS // 2, 2, seq, 3)
    parts = parts.transpose(0, 1, 3, 2, 4).reshape(bsz, C_HEADS // 2, seq, 6)
    k_aug = jnp.concatenate([jnp.ones((bsz, C_HEADS // 2, seq, 3), BF16), parts,
                             jnp.zeros((bsz, C_HEADS // 2, seq, LANE - 9), BF16)], axis=-1)
    y_c = _fox(p_main, k_aug, cum.reshape(bsz, C_HEADS, seq, 1), tb['fox_cb'], bsz, seq, tq=FOX_TQ)

    mkv = _memkv(mem, g_mem.reshape(1, -1), w_mem_kv.astype(BF16))
    wg = w_gate.reshape(D_MODEL, N_BRANCH * D_MODEL).astype(BF16)
    return _merge(x2, g_pre.reshape(1, -1), g_post.reshape(1, -1), y_a, y_b, y_c, p_main, mkv,
                  wg, w_br.astype(BF16), w_o.astype(BF16), seq, tz=min(512, seq))


def kernel(x, mem, w_in, g_pre, g_post, g_mem, w_mem_kv, a_ln_g, a_ln_b, a_ws, a_bs, b_cmp_pos_k, b_cmp_w1_k,
           b_cmp_w2_k, b_cmp_pos_v, b_cmp_w1_v, b_cmp_w2_v, c_fbias, w_br, w_gate, w_o):
    bsz, seq, d = x.shape
    tables = dict(**_rope_tables(seq), **_nsa_constants(seq, NSA_TQ), **_fcum_constants(seq),
                  fox_cb=_causal_bias(FOX_TQ, 2 * FOX_TQ))
    x2 = x.reshape(bsz * seq, d)
    for l in range(w_in.shape[0]):
        x2 = _layer(x2, mem, bsz, seq, tables, w_in[l], g_pre[l], g_post[l], g_mem[l], w_mem_kv[l],
                    a_ln_g[l], a_ln_b[l], a_ws[l], a_bs[l], b_cmp_pos_k[l], b_cmp_w1_k[l], b_cmp_w2_k[l],
                    b_cmp_pos_v[l], b_cmp_w1_v[l], b_cmp_w2_v[l], c_fbias[l], w_br[l], w_gate[l], w_o[l])
    return x2.reshape(bsz, seq, d)
```

```python
import functools

import numpy as np
import jax
import jax.numpy as jnp
from jax import lax
from jax.experimental import pallas as pl
from jax.experimental.pallas import tpu as pltpu

F32 = jnp.float32
BF16 = jnp.bfloat16

D_MODEL = 1024
N_BRANCH = 4
BRANCH_W = 512
A_GROUPS = 4
A_CHUNK = 128
B_HEADS = 8
B_KV = 2
B_GQ = B_HEADS // B_KV
B_HD = 64
CMP_LEN = 32
CMP_STRIDE = 16
SEL_LEN = 64
SEL_N = 8
WIN = 256
C_HEADS = 8
C_HD = 64
M_HEADS = 4
M_HD = BRANCH_W // M_HEADS
ROPE_THETA = 10000.0
EPS = 1e-6
NEG = -1e30
BIG = 1e9
LOG2E = 1.4426950408889634

IN_SIZES = (
    BRANCH_W, BRANCH_W, BRANCH_W,
    B_HEADS * B_HD, B_KV * B_HD, B_KV * B_HD, B_KV * B_HD,
    B_KV * B_HD, B_KV * B_HD, B_KV * B_HD, B_HEADS * 3, BRANCH_W,
    C_HEADS * C_HD, C_HEADS * C_HD, C_HEADS * C_HD, C_HEADS, BRANCH_W,
    M_HEADS * M_HD, BRANCH_W,
)
IN_SPLITS = tuple(int(c) for c in np.cumsum(IN_SIZES)[:-1])

LANE = 128
VMEM_LIMIT = 56 * 1024 * 1024

NSA_TQ = 256
FOX_TQ = 256
FOX_PAIRS = 4

COL_A_U, COL_A_V, COL_A_Z = 0, 4, 8
COL_B_Q, COL_B_Z = 12, 16
COL_C_Q, COL_C_K, COL_C_V, COL_C_Z = 20, 24, 28, 32
COL_M_Q, COL_M_Z = 36, 40
COL_KS, COL_KW, COL_VS, COL_VW = 44, 46, 48, 50
COL_KC, COL_VC = 52, 53
N_COLS = 54
ROPE_COLS = frozenset(list(range(COL_B_Q, COL_B_Q + 4)) + [COL_KS, COL_KS + 1, COL_KW, COL_KW + 1, COL_KC])
W_MAIN = N_COLS * LANE
SMALL_G0, SMALL_F0 = 0, B_HEADS * 3

_NT = (((1,), (1,)), ((), ()))
_TN = (((0,), (0,)), ((), ()))


def _dot(a, b):
    return jnp.dot(a, b, preferred_element_type=F32)


def _dot_nt(a, b):
    return lax.dot_general(a, b, _NT, preferred_element_type=F32)


def _dot_tn(a, b):
    return lax.dot_general(a, b, _TN, preferred_element_type=F32)


def _online_steps(score_fns, vs, m_sc, l_sc, acc_sc, depth=4):
    n = len(score_fns)
    sts = {}
    for t in range(n + depth):
        if t < n:
            sts[t] = score_fns[t]()
        g = t - depth
        if g < 0:
            continue
        st = sts.pop(g)
        m_old = m_sc[g]
        m_new = jnp.maximum(m_old, jnp.max(st, axis=0, keepdims=True))
        alpha = jnp.exp2(m_old - m_new)
        pe = jnp.exp2(st - m_new)
        l_sc[g] = alpha * l_sc[g] + jnp.sum(pe, axis=0, keepdims=True)
        m_sc[g] = m_new
        acc_sc[g] = alpha * acc_sc[g] + _dot_tn(vs[g], pe.astype(BF16))


def _split3(a):
    hi = a.astype(BF16)
    r1 = a - hi.astype(F32)
    mid = r1.astype(BF16)
    lo = (r1 - mid.astype(F32)).astype(BF16)
    return hi, mid, lo


def _rms(x, g):
    return x * lax.rsqrt(jnp.mean(x * x, axis=-1, keepdims=True) + EPS) * g


def _silu(x):
    return x * jax.nn.sigmoid(x)


def _const_spec(shape):
    nd = len(shape)
    return pl.BlockSpec(shape, lambda *_: (0,) * nd, pipeline_mode=pl.Buffered(1))


def _params(n_grid):
    return pltpu.CompilerParams(dimension_semantics=("arbitrary",) * n_grid, vmem_limit_bytes=VMEM_LIMIT)


def _proj_kernel(x_ref, g_ref, w_ref, ws_ref, cos_ref, sin_ref, o_ref, os_ref):
    h = _rms(x_ref[...], g_ref[...]).astype(BF16)
    tm = h.shape[0]
    lane = lax.broadcasted_iota(jnp.int32, (tm, LANE), 1)
    first_half = (lane & (B_HD - 1)) < (B_HD // 2)
    cos = cos_ref[...]
    sin = sin_ref[...]
    chunk = 2
    for c0 in range(0, N_COLS, chunk):
        acc = _dot(h, w_ref[:, c0 * LANE:(c0 + chunk) * LANE])
        for j in range(chunk):
            a = acc[:, j * LANE:(j + 1) * LANE]
            if c0 + j in ROPE_COLS:
                swapped = jnp.where(first_half, pltpu.roll(a, LANE - B_HD // 2, 1), pltpu.roll(a, B_HD // 2, 1))
                a = a * cos + swapped * sin
            o_ref[:, (c0 + j) * LANE:(c0 + j + 1) * LANE] = a.astype(BF16)
    os_ref[...] = _dot(h, ws_ref[...])


def _proj(x2, g_pre, w_main, w_small, cos_t, sin_t, seq, tm):
    n = x2.shape[0]
    nq = seq // tm
    return pl.pallas_call(
        _proj_kernel,
        grid=(n // tm,),
        in_specs=[
            pl.BlockSpec((tm, D_MODEL), lambda i: (i, 0)),
            _const_spec((1, D_MODEL)),
            _const_spec((D_MODEL, W_MAIN)),
            _const_spec((D_MODEL, LANE)),
            pl.BlockSpec((tm, LANE), lambda i: (i % nq, 0)),
            pl.BlockSpec((tm, LANE), lambda i: (i % nq, 0)),
        ],
        out_specs=[
            pl.BlockSpec((tm, W_MAIN), lambda i: (i, 0)),
            pl.BlockSpec((tm, LANE), lambda i: (i, 0)),
        ],
        out_shape=[jax.ShapeDtypeStruct((n, W_MAIN), BF16), jax.ShapeDtypeStruct((n, LANE), F32)],
        compiler_params=_params(1),
        name="proj",
    )(x2, g_pre, w_main, w_small, cos_t, sin_t)


def _gmlp_kernel(u_ref, v_ref, z_ref, lng_ref, lnb_ref, ws_ref, bs_ref, o_ref):
    ta = u_ref.shape[0]
    u = jax.nn.gelu(u_ref[...].astype(F32))
    v = jax.nn.gelu(v_ref[...].astype(F32))
    mu = jnp.mean(v, axis=-1, keepdims=True)
    vc = v - mu
    v = vc * lax.rsqrt(jnp.mean(vc * vc, axis=-1, keepdims=True) + EPS) * lng_ref[...] + lnb_ref[...]
    vb = v.astype(BF16)
    gate = _silu(z_ref[...].astype(F32))
    row = lax.broadcasted_iota(jnp.int32, (A_CHUNK, A_CHUNK), 0)
    col = lax.broadcasted_iota(jnp.int32, (A_CHUNK, A_CHUNK), 1)
    tri = col <= row
    for g in range(A_GROUPS):
        w = jnp.where(tri, ws_ref[g], 0.0).astype(BF16)
        b = bs_ref[:, g:g + 1]
        for c in range(ta // A_CHUNK):
            rs = slice(c * A_CHUNK, (c + 1) * A_CHUNK)
            gs = slice(g * LANE, (g + 1) * LANE)
            s = _dot(w, vb[rs, gs]) + b
            o_ref[rs, gs] = (u[rs, gs] * s * gate[rs, gs]).astype(BF16)


def _gmlp(p_main, ln_g, ln_b, ws, bs_t, ta):
    n = p_main.shape[0]
    blk = lambda c: pl.BlockSpec((ta, BRANCH_W), lambda i, c=c: (i, c))
    return pl.pallas_call(
        _gmlp_kernel,
        grid=(n // ta,),
        in_specs=[
            blk(COL_A_U // 4), blk(COL_A_V // 4), blk(COL_A_Z // 4),
            _const_spec((1, BRANCH_W)), _const_spec((1, BRANCH_W)),
            _const_spec((A_GROUPS, A_CHUNK, A_CHUNK)), _const_spec((A_CHUNK, A_GROUPS)),
        ],
        out_specs=pl.BlockSpec((ta, BRANCH_W), lambda i: (i, 0)),
        out_shape=jax.ShapeDtypeStruct((n, BRANCH_W), BF16),
        compiler_params=_params(1),
        name="gmlp",
    )(p_main, p_main, p_main, ln_g, ln_b, ws, bs_t)


def _compress_kernel(x_ref, pos_ref, w1_ref, w2_ref, o_ref):
    x = x_ref[0].astype(F32)
    top = _dot((x + pos_ref[0:1, :]).astype(BF16), w1_ref[0])
    bot = _dot((x + pos_ref[1:2, :]).astype(BF16), w1_ref[1])
    nrow = x.shape[0]
    a = top + pltpu.roll(bot, nrow - 1, 0)
    o_ref[0] = _dot(_silu(a).astype(BF16), w2_ref[...]).astype(BF16)


def _compress(x16, pos2, w1big, w2big):
    bsz, nrow, width = x16.shape
    return pl.pallas_call(
        _compress_kernel,
        grid=(bsz,),
        in_specs=[
            pl.BlockSpec((1, nrow, width), lambda b: (b, 0, 0)),
            _const_spec((2, width)),
            _const_spec((2, width, LANE)),
            _const_spec((LANE, 2 * LANE)),
        ],
        out_specs=pl.BlockSpec((1, nrow, 2 * LANE), lambda b: (b, 0, 0)),
        out_shape=jax.ShapeDtypeStruct((bsz, nrow, 2 * LANE), BF16),
        compiler_params=_params(1),
        name="compress",
    )(x16, pos2, w1big, w2big)


def _softmax_t(st, mask):
    sm = jnp.where(mask, st, NEG)
    m = jnp.max(sm, axis=0, keepdims=True)
    e = jnp.where(mask, jnp.exp2(sm - m), 0.0)
    return e, jnp.sum(e, axis=0, keepdims=True)


def _nsa_kernel(q_ref, ks_ref, vs_ref, kw_ref, vw_ref, kcmp_ref, vcmp_ref, glt_ref, ovt_ref, blk_ref, cb_ref, wb_ref,
                o_ref, qx_sc, m_sc, l_sc, acc_sc, part_sc, *, tq, seq):
    i = pl.program_id(1)
    qs = i * tq
    kc = tq
    scale = B_HD ** -0.5 * LOG2E
    nsel = seq // SEL_LEN
    nc = (seq - CMP_LEN) // CMP_STRIDE + 1
    nlane = B_GQ * tq
    lane = lax.broadcasted_iota(jnp.int32, (tq, LANE), 1)
    lo = lane < B_HD
    tpos = qs + (lax.broadcasted_iota(jnp.int32, (1, nlane), 1) & (tq - 1))
    ovt = ovt_ref[...]
    nselp = ovt.shape[0]
    j = lax.broadcasted_iota(jnp.int32, (nselp, tq), 0)
    cur = (qs + lax.broadcasted_iota(jnp.int32, (nselp, tq), 1)) // SEL_LEN
    forced = (j == 0) | (j == cur) | (j == cur - 1)
    future = j > cur

    qg, o_cmp = [], []
    for g in range(B_KV):
        qa = q_ref[:, 2 * g * LANE:(2 * g + 1) * LANE].astype(F32) * scale
        qb = q_ref[:, (2 * g + 1) * LANE:(2 * g + 2) * LANE].astype(F32) * scale
        q = jnp.concatenate([jnp.where(lo, qa, 0.0), jnp.where(lo, 0.0, qa),
                             jnp.where(lo, qb, 0.0), jnp.where(lo, 0.0, qb)], axis=0)
        q = q.T.astype(BF16)
        qg.append(q)

        st = _dot(kcmp_ref[0, :, g * LANE:(g + 1) * LANE], q)
        cpos = lax.broadcasted_iota(jnp.int32, st.shape, 0)
        ok = ((cpos * CMP_STRIDE + (CMP_LEN - 1)) <= tpos) & (cpos < nc)
        e, l = _softmax_t(st, ok)
        p = e / jnp.where(l > 0.0, l, 1.0)
        o_cmp.append(_dot_tn(vcmp_ref[0, :, g * LANE:g * LANE + B_HD], p.astype(BF16)))

        psum = p[:, 0:tq] + p[:, tq:2 * tq] + p[:, 2 * tq:3 * tq] + p[:, 3 * tq:4 * tq]
        hi = psum.astype(BF16)
        lo_part = (psum - hi.astype(F32)).astype(BF16)
        imp = _dot(ovt, hi) + _dot(ovt, lo_part)
        imp = jnp.where(forced, BIG, jnp.where(future, NEG, imp))
        imp = jnp.where(j < nsel, imp, -jnp.inf)
        rank = jnp.zeros((nselp, tq), F32)
        for ii in range(nsel):
            r = imp[ii:ii + 1, :]
            beats = (r > imp) | ((r == imp) & (ii < j))
            rank = rank + jnp.where(beats, 1.0, 0.0)
        bias_t = jnp.where((rank < float(min(SEL_N, nsel))) & (j < nsel), 0.0, NEG)
        bias_rows = jnp.concatenate([bias_t, jnp.zeros((LANE - nselp, tq), F32)], axis=0).astype(BF16)
        for r in range(B_GQ):
            qx_sc[g * B_GQ + r] = jnp.concatenate([q[:, r * tq:(r + 1) * tq], bias_rows], axis=0)

    m_sc[...] = jnp.full(m_sc.shape, NEG, F32)
    l_sc[...] = jnp.zeros(l_sc.shape, F32)
    acc_sc[...] = jnp.zeros(acc_sc.shape, F32)

    def step(k0, diagonal):
        fns, vs = [], []
        for g in range(B_KV):
            for r in range(B_GQ):
                def score(g=g, r=r):
                    kk = jnp.concatenate([ks_ref[pl.ds(k0, kc), g * LANE:(g + 1) * LANE],
                                          blk_ref[pl.ds(k0, kc), :]], axis=1)
                    st = _dot(kk, qx_sc[g * B_GQ + r])
                    return st + cb_ref[...] if diagonal else st
                fns.append(score)
                vs.append(vs_ref[pl.ds(k0, kc), g * LANE:g * LANE + B_HD])
        _online_steps(fns, vs, m_sc, l_sc, acc_sc)

    step(pl.multiple_of(qs, tq), True)

    wlen = min(WIN + tq, seq)
    w0 = pl.multiple_of(jnp.clip(qs - WIN, 0, seq - wlen), tq)
    wb = wb_ref[pl.ds(pl.multiple_of(WIN - (qs - w0), tq), wlen), :]
    g_all = jax.nn.sigmoid(glt_ref[0])
    gate = lambda g, r, n: g_all[g, 3 * r + n:3 * r + n + 1, :]
    for g in range(B_KV):
        st = _dot(kw_ref[pl.ds(w0, wlen), g * LANE:(g + 1) * LANE], qg[g]) + wb
        e = jnp.exp2(st - jnp.max(st, axis=0, keepdims=True))
        o_win = _dot_tn(vw_ref[pl.ds(w0, wlen), g * LANE:g * LANE + B_HD], e.astype(BF16))
        o_win = o_win / jnp.sum(e, axis=0, keepdims=True)
        for r in range(B_GQ):
            cs = slice(r * tq, (r + 1) * tq)
            part_sc[g, :, cs] = gate(g, r, 0) * o_cmp[g][:, cs] + gate(g, r, 2) * o_win[:, cs]

    def body(c, carry):
        step(pl.multiple_of(c * kc, kc), False)
        return carry

    lax.fori_loop(0, i, body, 0)

    for g in range(B_KV):
        outs = []
        for r in range(B_GQ):
            cs = slice(r * tq, (r + 1) * tq)
            o_sel = acc_sc[g * B_GQ + r] / l_sc[g * B_GQ + r]
            outs.append(part_sc[g, :, cs] + gate(g, r, 1) * o_sel)
        o_ref[:, 2 * g * LANE:(2 * g + 1) * LANE] = jnp.concatenate(outs[0:2], axis=0).T.astype(BF16)
        o_ref[:, (2 * g + 1) * LANE:(2 * g + 2) * LANE] = jnp.concatenate(outs[2:4], axis=0).T.astype(BF16)


def _nsa(p_main, cmp_k, cmp_v, gate_logits_t, ovt, block_onehot, causal_bias, window_bias, bsz, seq, tq):
    n = p_main.shape[0]
    nq = seq // tq
    kv_spec = lambda col: pl.BlockSpec((seq, B_KV * LANE), lambda b, i, col=col: (b, col // B_KV))
    nrow = cmp_k.shape[1]
    cmp_spec = pl.BlockSpec((1, nrow, B_KV * LANE), lambda b, i: (b, 0, 0))
    nlane = B_GQ * tq
    return pl.pallas_call(
        functools.partial(_nsa_kernel, tq=tq, seq=seq),
        grid=(bsz, nq),
        in_specs=[
            pl.BlockSpec((tq, BRANCH_W), lambda b, i: (b * nq + i, COL_B_Q // 4)),
            kv_spec(COL_KS), kv_spec(COL_VS), kv_spec(COL_KW), kv_spec(COL_VW),
            cmp_spec, cmp_spec,
            pl.BlockSpec((1, B_KV, 3 * B_GQ, tq), lambda b, i: (b, 0, 0, i)),
            _const_spec(ovt.shape), _const_spec(block_onehot.shape), _const_spec(causal_bias.shape),
            _const_spec(window_bias.shape),
        ],
        out_specs=pl.BlockSpec((tq, BRANCH_W), lambda b, i: (b * nq + i, 0)),
        out_shape=jax.ShapeDtypeStruct((n, BRANCH_W), BF16),
        scratch_shapes=[pltpu.VMEM((B_HEADS, 2 * LANE, tq), BF16), pltpu.VMEM((B_HEADS, 1, tq), F32),
                        pltpu.VMEM((B_HEADS, 1, tq), F32), pltpu.VMEM((B_HEADS, B_HD, tq), F32),
                        pltpu.VMEM((B_KV, B_HD, nlane), F32)],
        compiler_params=_params(2),
        name="nsa",
    )(p_main, p_main, p_main, p_main, p_main, cmp_k, cmp_v, gate_logits_t, ovt, block_onehot, causal_bias, window_bias)


def _fcum_kernel(f_ref, b_ref, u_ref, m_ref, o_ref, kaug_ref):
    x = f_ref[0] + b_ref[...]
    lf = jnp.minimum(x, 0.0) - jnp.log(1.0 + jnp.exp(-jnp.abs(x)))
    u = u_ref[...]
    within = sum(_dot(t, u) for t in _split3(lf))
    mm = m_ref[...]
    before = sum(_dot(mm, t) for t in _split3(within))
    c = (within + before[:, LANE - 1:LANE]) * LOG2E
    o_ref[0] = c
    hi, mid, lo = (t.astype(F32) for t in _split3(c))
    ones = jnp.ones((C_HEADS, LANE), F32)
    pad = jnp.zeros((LANE - 4 * C_HEADS, LANE), F32)
    for t in range(c.shape[0] // C_HEADS):
        rs = slice(t * C_HEADS, (t + 1) * C_HEADS)
        stacked = jnp.concatenate([hi[rs], mid[rs], lo[rs], ones, pad], axis=0)
        kaug_ref[0, t * LANE:(t + 1) * LANE, :] = stacked.T.astype(BF16)


def _fcum(f_tiles, bias_col, umat, mmat):
    bsz, nrow, _ = f_tiles.shape
    seq = nrow // C_HEADS * LANE
    spec = pl.BlockSpec((1, nrow, LANE), lambda b: (b, 0, 0))
    return pl.pallas_call(
        _fcum_kernel,
        grid=(bsz,),
        in_specs=[spec, _const_spec((nrow, 1)), _const_spec((LANE, LANE)), _const_spec((nrow, nrow))],
        out_specs=[spec, pl.BlockSpec((1, seq, LANE), lambda b: (b, 0, 0))],
        out_shape=[jax.ShapeDtypeStruct((bsz, nrow, LANE), F32), jax.ShapeDtypeStruct((bsz, seq, LANE), BF16)],
        compiler_params=_params(1),
        name="fcum",
    )(f_tiles, bias_col, umat, mmat)


def _fox_kernel(q_ref, k_ref, v_ref, kaug_ref, crow_ref, cb_ref, o_ref, m_sc, l_sc, acc_sc, *, tq):
    i = pl.program_id(1)
    kc = tq
    scale = C_HD ** -0.5 * LOG2E
    row = lax.broadcasted_iota(jnp.int32, (LANE, tq), 0)
    top = row < C_HD
    crow = crow_ref[0]
    qxs = []
    for pr in range(FOX_PAIRS):
        q2t = (q_ref[:, pr * LANE:(pr + 1) * LANE].astype(F32) * scale).T
        halves = []
        for hd in range(2):
            h = 2 * pr + hd
            ct = jnp.concatenate([crow[t * C_HEADS + h:t * C_HEADS + h + 1, :] for t in range(tq // LANE)], axis=1)
            hi, mid, lw = (t.astype(F32) for t in _split3(ct))
            aug = jnp.where((row == h) | (row == C_HEADS + h) | (row == 2 * C_HEADS + h), -1.0, jnp.where(
                row == 3 * C_HEADS, hi, jnp.where(row == 3 * C_HEADS + 1, mid, jnp.where(
                    row == 3 * C_HEADS + 2, lw, 0.0))))
            qh = jnp.where(top, q2t, 0.0) if hd == 0 else jnp.where(top, 0.0, q2t)
            halves.append(jnp.concatenate([qh, aug], axis=0))
        qxs.append(jnp.concatenate(halves, axis=1).astype(BF16))

    m_sc[...] = jnp.full(m_sc.shape, NEG, F32)
    l_sc[...] = jnp.zeros(l_sc.shape, F32)
    acc_sc[...] = jnp.zeros(acc_sc.shape, F32)

    def step(c, diagonal):
        k0 = pl.multiple_of(c * kc, kc)
        fns = []
        for pr in range(FOX_PAIRS):
            cols = slice(pr * LANE, (pr + 1) * LANE)
            for hd in range(2):
                def score(pr=pr, hd=hd, cols=cols):
                    kx = jnp.concatenate([k_ref[pl.ds(k0, kc), cols], kaug_ref[0, pl.ds(k0, kc), :]], axis=1)
                    st = _dot(kx, qxs[pr][:, hd * tq:(hd + 1) * tq])
                    return st + cb_ref[...] if diagonal else st
                fns.append(score)
        vs = [v_ref[pl.ds(k0, kc), hh * C_HD:(hh + 1) * C_HD] for hh in range(2 * FOX_PAIRS)]
        _online_steps(fns, vs, m_sc, l_sc, acc_sc)

    def body(c, carry):
        step(c, False)
        return carry

    lax.fori_loop(0, i, body, 0)
    step(i, True)

    for pr in range(FOX_PAIRS):
        ot = jnp.concatenate([acc_sc[2 * pr + hd] / l_sc[2 * pr + hd] for hd in range(2)], axis=0)
        o_ref[:, pr * LANE:(pr + 1) * LANE] = ot.T.astype(BF16)


def _fox(p_main, k_aug, c_rows, causal_bias, bsz, seq, tq):
    assert 2 * FOX_PAIRS == C_HEADS
    n = p_main.shape[0]
    nq = seq // tq
    return pl.pallas_call(
        functools.partial(_fox_kernel, tq=tq),
        grid=(bsz, nq),
        in_specs=[
            pl.BlockSpec((tq, BRANCH_W), lambda b, i: (b * nq + i, COL_C_Q // 4)),
            pl.BlockSpec((seq, BRANCH_W), lambda b, i: (b, COL_C_K // 4)),
            pl.BlockSpec((seq, BRANCH_W), lambda b, i: (b, COL_C_V // 4)),
            pl.BlockSpec((1, seq, LANE), lambda b, i: (b, 0, 0)),
            pl.BlockSpec((1, tq // LANE * C_HEADS, LANE), lambda b, i: (b, i, 0)),
            _const_spec(causal_bias.shape),
        ],
        out_specs=pl.BlockSpec((tq, BRANCH_W), lambda b, i: (b * nq + i, 0)),
        out_shape=jax.ShapeDtypeStruct((n, BRANCH_W), BF16),
        scratch_shapes=[pltpu.VMEM((C_HEADS, 1, tq), F32), pltpu.VMEM((C_HEADS, 1, tq), F32),
                        pltpu.VMEM((C_HEADS, C_HD, tq), F32)],
        compiler_params=_params(2),
        name="fox",
    )(p_main, p_main, p_main, k_aug, c_rows, causal_bias)


def _memkv_kernel(mem_ref, g_ref, w_ref, o_ref):
    h = _rms(mem_ref[0], g_ref[...]).astype(BF16)
    o_ref[0] = _dot(h, w_ref[...]).astype(BF16)


def _memkv(mem, g_mem, w_kv):
    bsz, mlen, _ = mem.shape
    return pl.pallas_call(
        _memkv_kernel,
        grid=(bsz,),
        in_specs=[
            pl.BlockSpec((1, mlen, D_MODEL), lambda b: (b, 0, 0)),
            _const_spec((1, D_MODEL)), _const_spec((D_MODEL, 2 * BRANCH_W)),
        ],
        out_specs=pl.BlockSpec((1, mlen, 2 * BRANCH_W), lambda b: (b, 0, 0)),
        out_shape=jax.ShapeDtypeStruct((bsz, mlen, 2 * BRANCH_W), BF16),
        compiler_params=_params(1),
        name="memkv",
    )(mem, g_mem, w_kv)


def _merge_kernel(x_ref, gpre_ref, gpost_ref, ya_ref, yb_ref, yc_ref, bz_ref, cz_ref, mq_ref, mz_ref, mkv_ref,
                  wg_ref, wbr_ref, wo_ref, o_ref):
    x = x_ref[...]
    h = _rms(x, gpre_ref[...]).astype(BF16)

    mq = mq_ref[...]
    scale = M_HD ** -0.5
    ym = []
    for hd in range(M_HEADS):
        mk = mkv_ref[0, :, hd * M_HD:(hd + 1) * M_HD]
        mv = mkv_ref[0, :, BRANCH_W + hd * M_HD:BRANCH_W + (hd + 1) * M_HD]
        s = _dot_nt(mq[:, hd * M_HD:(hd + 1) * M_HD], mk) * scale
        e = jnp.exp(s - jnp.max(s, axis=-1, keepdims=True))
        p = e / jnp.sum(e, axis=-1, keepdims=True)
        ym.append(_dot(p.astype(BF16), mv))
    y_m = jnp.concatenate(ym, axis=-1)

    ys = (
        ya_ref[...],
        (yb_ref[...].astype(F32) * _silu(bz_ref[...].astype(F32))).astype(BF16),
        (yc_ref[...].astype(F32) * _silu(cz_ref[...].astype(F32))).astype(BF16),
        (y_m * _silu(mz_ref[...].astype(F32))).astype(BF16),
    )
    merged = None
    for nb in range(N_BRANCH):
        gate = jax.nn.sigmoid(_dot(h, wg_ref[:, nb * D_MODEL:(nb + 1) * D_MODEL]))
        term = gate * _dot(ys[nb], wbr_ref[nb])
        merged = term if merged is None else merged + term
    out = _dot(merged.astype(BF16), wo_ref[...])
    o_ref[...] = x + _rms(out, gpost_ref[...])


def _merge(x2, g_pre, g_post, y_a, y_b, y_c, p_main, mkv, w_gate, w_br, w_o, seq, tz):
    n = x2.shape[0]
    nq = seq // tz
    mlen = mkv.shape[1]
    act = lambda c: pl.BlockSpec((tz, BRANCH_W), lambda i, c=c: (i, c))
    return pl.pallas_call(
        _merge_kernel,
        grid=(n // tz,),
        in_specs=[
            pl.BlockSpec((tz, D_MODEL), lambda i: (i, 0)),
            _const_spec((1, D_MODEL)), _const_spec((1, D_MODEL)),
            act(0), act(0), act(0),
            act(COL_B_Z // 4), act(COL_C_Z // 4), act(COL_M_Q // 4), act(COL_M_Z // 4),
            pl.BlockSpec((1, mlen, 2 * BRANCH_W), lambda i: (i // nq, 0, 0)),
            _const_spec((D_MODEL, N_BRANCH * D_MODEL)),
            _const_spec((N_BRANCH, BRANCH_W, D_MODEL)),
            _const_spec((D_MODEL, D_MODEL)),
        ],
        out_specs=pl.BlockSpec((tz, D_MODEL), lambda i: (i, 0)),
        out_shape=jax.ShapeDtypeStruct((n, D_MODEL), F32),
        compiler_params=_params(1),
        name="merge",
    )(x2, g_pre, g_post, y_a, y_b, y_c, p_main, p_main, p_main, p_main, mkv, w_gate, w_br, w_o)


def _pack_w_in(w):
    (a_u, a_v, a_z, b_q, b_kc, b_vc, b_ks, b_vs, b_kw, b_vw, b_g, b_z,
     c_q, c_k, c_v, c_f, c_z, m_q, m_z) = jnp.split(w, IN_SPLITS, axis=1)
    dup = lambda t: jnp.concatenate([t[:, :B_HD], t[:, :B_HD], t[:, B_HD:], t[:, B_HD:]], axis=1)
    main = jnp.concatenate([a_u, a_v, a_z, b_q, b_z, c_q, c_k, c_v, c_z, m_q, m_z,
                            dup(b_ks), dup(b_kw), dup(b_vs), dup(b_vw), b_kc, b_vc], axis=1)
    small = jnp.concatenate([b_g, c_f, jnp.zeros((w.shape[0], LANE - b_g.shape[1] - c_f.shape[1]), w.dtype)], axis=1)
    return main.astype(BF16), small.astype(BF16)


def _pack_compress(pos, w1, w2):
    eye = jnp.eye(B_KV, dtype=F32)
    w1r = w1.reshape(2, CMP_STRIDE, B_HD, B_HD)
    w1big = jnp.einsum('hlde,gk->hlgdke', w1r, eye).reshape(2, CMP_STRIDE * B_KV * B_HD, B_KV * B_HD)
    w2big = jnp.einsum('de,gk,u->gdkue', w2, eye, jnp.ones((2,), F32)).reshape(B_KV * B_HD, 2 * B_KV * B_HD)
    pos2 = jnp.broadcast_to(pos.reshape(2, CMP_STRIDE, 1, B_HD), (2, CMP_STRIDE, B_KV, B_HD)).reshape(2, -1)
    return pos2.astype(F32), w1big.astype(BF16), w2big.astype(BF16)


def _rope_tables(seq):
    half = B_HD // 2
    freqs = ROPE_THETA ** (-jnp.arange(half, dtype=F32) / half)
    ang = jnp.arange(seq, dtype=F32)[:, None] * freqs[None, :]
    cos, sin = jnp.cos(ang), jnp.sin(ang)
    cos_t = jnp.concatenate([cos, cos, cos, cos], axis=1)
    sin_t = jnp.concatenate([-sin, sin, -sin, sin], axis=1)
    return dict(cos=cos_t, sin=sin_t)


def _causal_bias(tq, nlane):
    k = np.arange(tq)[:, None]
    t = (np.arange(nlane) & (tq - 1))[None, :]
    return jnp.asarray(np.where(k <= t, 0.0, NEG), F32)


def _nsa_constants(seq, tq):
    nc = (seq - CMP_LEN) // CMP_STRIDE + 1
    nsel = seq // SEL_LEN
    nselp = -(-nsel // 8) * 8
    ncp = seq // CMP_STRIDE
    ci = np.arange(ncp) * CMP_STRIDE
    sj = np.arange(nselp) * SEL_LEN
    ovt = ((ci[None, :] <= sj[:, None] + SEL_LEN - 1) & (ci[None, :] + CMP_LEN - 1 >= sj[:, None])
           & (np.arange(ncp)[None, :] < nc) & (np.arange(nselp)[:, None] < nsel))
    nlane = B_GQ * tq
    r = np.arange(2 * WIN + tq)[:, None]
    t = (np.arange(nlane) & (tq - 1))[None, :]
    wb = np.where((r > t) & (r <= t + WIN), 0.0, NEG)
    onehot = (np.arange(seq)[:, None] // SEL_LEN) == np.arange(LANE)[None, :]
    return dict(ovt=jnp.asarray(ovt, BF16), nsa_blk=jnp.asarray(onehot, BF16), nsa_cb=_causal_bias(tq, tq),
                nsa_wb=jnp.asarray(wb, F32))


def _fcum_constants(seq):
    ntile = seq // LANE
    nrow = C_HEADS * ntile
    umat = np.arange(LANE)[:, None] <= np.arange(LANE)[None, :]
    r = np.arange(nrow)
    mmat = (r[:, None] % C_HEADS == r[None, :] % C_HEADS) & (r[None, :] // C_HEADS < r[:, None] // C_HEADS)
    return dict(umat=jnp.asarray(umat, BF16), mmat=jnp.asarray(mmat, BF16))


def _layer(x2, mem, bsz, seq, tb, w_in, g_pre, g_post, g_mem, w_mem_kv, a_ln_g, a_ln_b, a_ws, a_bs,
           pos_k, w1_k, w2_k, pos_v, w1_v, w2_v, c_fbias, w_br, w_gate, w_o):
    w_main, w_small = _pack_w_in(w_in)
    p_main, p_small = _proj(x2, g_pre.reshape(1, -1), w_main, w_small, tb['cos'], tb['sin'], seq, tm=min(512, seq))

    y_a = _gmlp(p_main, a_ln_g.reshape(1, -1), a_ln_b.reshape(1, -1), a_ws, a_bs.T, ta=min(512, seq))

    to16 = lambda col: p_main[:, col * LANE:(col + 1) * LANE].reshape(bsz, seq // CMP_STRIDE, CMP_STRIDE * LANE)
    cmp_k = _compress(to16(COL_KC), *_pack_compress(pos_k, w1_k, w2_k))
    cmp_v = _compress(to16(COL_VC), *_pack_compress(pos_v, w1_v, w2_v))
    glt = p_small[:, SMALL_G0:SMALL_G0 + 3 * B_HEADS].reshape(bsz, seq, B_KV, 3 * B_GQ).transpose(0, 2, 3, 1)
    y_b = _nsa(p_main, cmp_k, cmp_v, glt, tb['ovt'], tb['nsa_blk'], tb['nsa_cb'], tb['nsa_wb'], bsz, seq, tq=NSA_TQ)

    ntile = seq // LANE
    f_tiles = (p_small[:, SMALL_F0:SMALL_F0 + C_HEADS].reshape(bsz, ntile, LANE, C_HEADS).transpose(0, 1, 3, 2)
               .reshape(bsz, ntile * C_HEADS, LANE))
    bias_col = jnp.tile(c_fbias.astype(F32), ntile).reshape(-1, 1)
    c_rows, k_aug = _fcum(f_tiles, bias_col, tb['umat'], tb['mmat'])
    y_c = _fox(p_main, k_aug, c_rows, tb['fox_cb'], bsz, seq, tq=FOX_TQ)

    mkv = _memkv(mem, g_mem.reshape(1, -1), w_mem_kv.astype(BF16))
    wg = w_gate.reshape(D_MODEL, N_BRANCH * D_MODEL).astype(BF16)
    return _merge(x2, g_pre.reshape(1, -1), g_post.reshape(1, -1), y_a, y_b, y_c, p_main, mkv,
                  wg, w_br.astype(BF16), w_o.astype(BF16), seq, tz=min(512, seq))


def kernel(x, mem, w_in, g_pre, g_post, g_mem, w_mem_kv, a_ln_g, a_ln_b, a_ws, a_bs, b_cmp_pos_k, b_cmp_w1_k,
           b_cmp_w2_k, b_cmp_pos_v, b_cmp_w1_v, b_cmp_w2_v, c_fbias, w_br, w_gate, w_o):
    bsz, seq, d = x.shape
    tables = dict(**_rope_tables(seq), **_nsa_constants(seq, NSA_TQ), **_fcum_constants(seq),
                  fox_cb=_causal_bias(FOX_TQ, FOX_TQ))
    x2 = x.reshape(bsz * seq, d)
    for l in range(w_in.shape[0]):
        x2 = _layer(x2, mem, bsz, seq, tables, w_in[l], g_pre[l], g_post[l], g_mem[l], w_mem_kv[l],
                    a_ln_g[l], a_ln_b[l], a_ws[l], a_bs[l], b_cmp_pos_k[l], b_cmp_w1_k[l], b_cmp_w2_k[l],
                    b_cmp_pos_v[l], b_cmp_w1_v[l], b_cmp_w2_v[l], c_fbias[l], w_br[l], w_gate[l], w_o[l])
    return x2.reshape(bsz, seq, d)
```

```python
import functools

import numpy as np
import jax
import jax.numpy as jnp
from jax import lax
from jax.experimental import pallas as pl
from jax.experimental.pallas import tpu as pltpu

F32 = jnp.float32
BF16 = jnp.bfloat16

D_MODEL = 1024
N_BRANCH = 4
BRANCH_W = 512
A_GROUPS = 4
A_CHUNK = 128
B_HEADS = 8
B_KV = 2
B_GQ = B_HEADS // B_KV
B_HD = 64
CMP_LEN = 32
CMP_STRIDE = 16
SEL_LEN = 64
SEL_N = 8
WIN = 256
C_HEADS = 8
C_HD = 64
M_HEADS = 4
M_HD = BRANCH_W // M_HEADS
ROPE_THETA = 10000.0
EPS = 1e-6
NEG = -1e30
BIG = 1e9
LOG2E = 1.4426950408889634

IN_SIZES = (
    BRANCH_W, BRANCH_W, BRANCH_W,
    B_HEADS * B_HD, B_KV * B_HD, B_KV * B_HD, B_KV * B_HD,
    B_KV * B_HD, B_KV * B_HD, B_KV * B_HD, B_HEADS * 3, BRANCH_W,
    C_HEADS * C_HD, C_HEADS * C_HD, C_HEADS * C_HD, C_HEADS, BRANCH_W,
    M_HEADS * M_HD, BRANCH_W,
)
IN_SPLITS = tuple(int(c) for c in np.cumsum(IN_SIZES)[:-1])

LANE = 128
VMEM_LIMIT = 56 * 1024 * 1024

NSA_TQ = 256
FOX_TQ = 256
FOX_PAIRS = 4

N_A = 12
COL_B_Q, COL_B_Z = 0, 4
COL_C_Q, COL_C_K, COL_C_V, COL_C_Z = 8, 12, 16, 20
COL_M_Q, COL_M_Z = 24, 28
COL_KS, COL_KW, COL_VS, COL_VW = 32, 34, 36, 38
N_COLS = 40
ROPE_COLS = frozenset(list(range(COL_B_Q, COL_B_Q + 4)) + [COL_KS, COL_KS + 1, COL_KW, COL_KW + 1])
W_MAIN = N_COLS * LANE
SMALL_G0, SMALL_F0 = 0, B_HEADS * 3

_NT = (((1,), (1,)), ((), ()))
_TN = (((0,), (0,)), ((), ()))


def _dot(a, b):
    return jnp.dot(a, b, preferred_element_type=F32)


def _dot_nt(a, b):
    return lax.dot_general(a, b, _NT, preferred_element_type=F32)


def _dot_tn(a, b):
    return lax.dot_general(a, b, _TN, preferred_element_type=F32)


def _online_steps(score_fns, vs, m_sc, l_sc, acc_sc, depth=4):
    n = len(score_fns)
    sts = {}
    for t in range(n + depth):
        if t < n:
            sts[t] = score_fns[t]()
        g = t - depth
        if g < 0:
            continue
        st = sts.pop(g)
        m_old = m_sc[g]
        m_new = jnp.maximum(m_old, jnp.max(st, axis=0, keepdims=True))
        alpha = jnp.exp2(m_old - m_new)
        pe = jnp.exp2(st - m_new)
        l_sc[g] = alpha * l_sc[g] + jnp.sum(pe, axis=0, keepdims=True)
        m_sc[g] = m_new
        acc_sc[g] = alpha * acc_sc[g] + _dot_tn(vs[g], pe.astype(BF16))


def _split3(a):
    hi = a.astype(BF16)
    r1 = a - hi.astype(F32)
    mid = r1.astype(BF16)
    lo = (r1 - mid.astype(F32)).astype(BF16)
    return hi, mid, lo


def _rms(x, g):
    return x * lax.rsqrt(jnp.mean(x * x, axis=-1, keepdims=True) + EPS) * g


def _silu(x):
    return x * jax.nn.sigmoid(x)


def _const_spec(shape):
    nd = len(shape)
    return pl.BlockSpec(shape, lambda *_: (0,) * nd, pipeline_mode=pl.Buffered(1))


def _params(n_grid):
    return pltpu.CompilerParams(dimension_semantics=("arbitrary",) * n_grid, vmem_limit_bytes=VMEM_LIMIT)


def _gmlp(u, v, z, lng_ref, lnb_ref, ws_ref, bs_ref, o_ref):
    ta = u.shape[0]
    u = jax.nn.gelu(u)
    v = jax.nn.gelu(v)
    mu = jnp.mean(v, axis=-1, keepdims=True)
    vc = v - mu
    v = vc * lax.rsqrt(jnp.mean(vc * vc, axis=-1, keepdims=True) + EPS) * lng_ref[...] + lnb_ref[...]
    vb = v.astype(BF16)
    gate = _silu(z)
    row = lax.broadcasted_iota(jnp.int32, (A_CHUNK, A_CHUNK), 0)
    col = lax.broadcasted_iota(jnp.int32, (A_CHUNK, A_CHUNK), 1)
    tri = col <= row
    for g in range(A_GROUPS):
        w = jnp.where(tri, ws_ref[g], 0.0).astype(BF16)
        b = bs_ref[:, g:g + 1]
        for c in range(ta // A_CHUNK):
            rs = slice(c * A_CHUNK, (c + 1) * A_CHUNK)
            gs = slice(g * LANE, (g + 1) * LANE)
            s = _dot(w, vb[rs, gs]) + b
            o_ref[rs, gs] = (u[rs, gs] * s * gate[rs, gs]).astype(BF16)


def _proj_kernel(x_ref, g_ref, w_ref, ws_ref, wc_ref, cos_ref, sin_ref, lng_ref, lnb_ref, aws_ref, abs_ref,
                 o_ref, os_ref, kc_ref, vc_ref, ya_ref):
    h = _rms(x_ref[...], g_ref[...]).astype(BF16)
    tm = h.shape[0]
    lane = lax.broadcasted_iota(jnp.int32, (tm, LANE), 1)
    first_half = (lane & (B_HD - 1)) < (B_HD // 2)
    cos = cos_ref[...]
    sin = sin_ref[...]
    def rope(a):
        swapped = jnp.where(first_half, pltpu.roll(a, LANE - B_HD // 2, 1), pltpu.roll(a, B_HD // 2, 1))
        return a * cos + swapped * sin

    chunk = 2
    slabs = [_dot(h, w_ref[:, c0 * LANE:(c0 + chunk) * LANE]) for c0 in range(0, N_A, chunk)]
    u, v, z = (jnp.concatenate(slabs[2 * k:2 * k + 2], axis=1) for k in range(3))
    _gmlp(u, v, z, lng_ref, lnb_ref, aws_ref, abs_ref, ya_ref)
    for c0 in range(0, N_COLS, chunk):
        acc = _dot(h, w_ref[:, (N_A + c0) * LANE:(N_A + c0 + chunk) * LANE])
        for j in range(chunk):
            a = acc[:, j * LANE:(j + 1) * LANE]
            o_ref[:, (c0 + j) * LANE:(c0 + j + 1) * LANE] = (rope(a) if c0 + j in ROPE_COLS else a).astype(BF16)
    acc = _dot(h, wc_ref[...])
    kc_ref[...] = rope(acc[:, 0:LANE]).reshape(kc_ref.shape)
    vc_ref[...] = acc[:, LANE:2 * LANE].reshape(vc_ref.shape)
    os_ref[...] = _dot_nt(ws_ref[...], h)


def _proj(x2, g_pre, w_main, w_small, w_cmp, cos_t, sin_t, ln_g, ln_b, a_ws, a_bs_t, seq, tm):
    n = x2.shape[0]
    nq = seq // tm
    return pl.pallas_call(
        _proj_kernel,
        grid=(n // tm,),
        in_specs=[
            pl.BlockSpec((tm, D_MODEL), lambda i: (i, 0)),
            _const_spec((1, D_MODEL)),
            _const_spec((D_MODEL, (N_A + N_COLS) * LANE)),
            _const_spec((LANE, D_MODEL)),
            _const_spec((D_MODEL, 2 * LANE)),
            pl.BlockSpec((tm, LANE), lambda i: (i % nq, 0)),
            pl.BlockSpec((tm, LANE), lambda i: (i % nq, 0)),
            _const_spec((1, BRANCH_W)), _const_spec((1, BRANCH_W)),
            _const_spec((A_GROUPS, A_CHUNK, A_CHUNK)), _const_spec((A_CHUNK, A_GROUPS)),
        ],
        out_specs=[
            pl.BlockSpec((tm, W_MAIN), lambda i: (i, 0)),
            pl.BlockSpec((LANE, tm), lambda i: (0, i)),
            pl.BlockSpec((tm // CMP_STRIDE, CMP_STRIDE, LANE), lambda i: (i, 0, 0)),
            pl.BlockSpec((tm // CMP_STRIDE, CMP_STRIDE, LANE), lambda i: (i, 0, 0)),
            pl.BlockSpec((tm, BRANCH_W), lambda i: (i, 0)),
        ],
        out_shape=[jax.ShapeDtypeStruct((n, W_MAIN), BF16), jax.ShapeDtypeStruct((LANE, n), F32),
                   jax.ShapeDtypeStruct((n // CMP_STRIDE, CMP_STRIDE, LANE), F32),
                   jax.ShapeDtypeStruct((n // CMP_STRIDE, CMP_STRIDE, LANE), F32),
                   jax.ShapeDtypeStruct((n, BRANCH_W), BF16)],
        compiler_params=_params(1),
        name="proj",
    )(x2, g_pre, w_main, w_small, w_cmp, cos_t, sin_t, ln_g, ln_b, a_ws, a_bs_t)


def _compress_kernel(xk_ref, xv_ref, posk_ref, posv_ref, w1k_ref, w1v_ref, w2k_ref, w2v_ref, ok_ref, ov_ref):
    def one(x_ref, pos_ref, w1_ref, w2_ref, o_ref):
        nrow = x_ref.shape[0]
        x = jnp.concatenate([x_ref[:, l, :] for l in range(CMP_STRIDE)], axis=1)
        top = _dot((x + pos_ref[0:1, :]).astype(BF16), w1_ref[0])
        bot = _dot((x + pos_ref[1:2, :]).astype(BF16), w1_ref[1])
        a = top + pltpu.roll(bot, nrow - 1, 0)
        o_ref[0] = _dot(_silu(a).astype(BF16), w2_ref[...]).astype(BF16)

    one(xk_ref, posk_ref, w1k_ref, w2k_ref, ok_ref)
    one(xv_ref, posv_ref, w1v_ref, w2v_ref, ov_ref)


def _compress(xk, xv, pk, pv, bsz):
    nrow = xk.shape[0] // bsz
    xspec = pl.BlockSpec((nrow, CMP_STRIDE, LANE), lambda b: (b, 0, 0))
    ospec = pl.BlockSpec((1, nrow, 2 * LANE), lambda b: (b, 0, 0))
    pos_spec, w1_spec, w2_spec = (_const_spec((2, CMP_STRIDE * LANE)), _const_spec((2, CMP_STRIDE * LANE, LANE)),
                                  _const_spec((LANE, 2 * LANE)))
    return pl.pallas_call(
        _compress_kernel,
        grid=(bsz,),
        in_specs=[xspec, xspec, pos_spec, pos_spec, w1_spec, w1_spec, w2_spec, w2_spec],
        out_specs=[ospec, ospec],
        out_shape=[jax.ShapeDtypeStruct((bsz, nrow, 2 * LANE), BF16)] * 2,
        compiler_params=_params(1),
        name="compress",
    )(xk, xv, pk[0], pv[0], pk[1], pv[1], pk[2], pv[2])


def _softmax_t(st, mask):
    sm = jnp.where(mask, st, NEG)
    m = jnp.max(sm, axis=0, keepdims=True)
    e = jnp.where(mask, jnp.exp2(sm - m), 0.0)
    return e, jnp.sum(e, axis=0, keepdims=True)


def _nsa_kernel(q_ref, ks_ref, vs_ref, kw_ref, vw_ref, kcmp_ref, vcmp_ref, glt_ref, ovt_ref, blk_ref, cb_ref, wb_ref,
                cm_ref, o_ref, qx_sc, m_sc, l_sc, acc_sc, part_sc, *, tq, seq):
    i = pl.program_id(1)
    qs = i * tq
    kc = tq
    scale = B_HD ** -0.5 * LOG2E
    nsel = seq // SEL_LEN
    nc = (seq - CMP_LEN) // CMP_STRIDE + 1
    nlane = B_GQ * tq
    lane = lax.broadcasted_iota(jnp.int32, (tq, LANE), 1)
    lo = lane < B_HD
    tpos = qs + (lax.broadcasted_iota(jnp.int32, (1, nlane), 1) & (tq - 1))
    ovt = ovt_ref[...]
    nselp = ovt.shape[0]
    j = lax.broadcasted_iota(jnp.int32, (nselp, tq), 0)
    cur = (qs + lax.broadcasted_iota(jnp.int32, (nselp, tq), 1)) // SEL_LEN
    forced = (j == 0) | (j == cur) | (j == cur - 1)
    future = j > cur
    j8 = lax.broadcasted_iota(jnp.int32, (8, tq), 0)
    ncp = kcmp_ref.shape[1]
    cmask = cm_ref[pl.ds(pl.multiple_of(cm_ref.shape[0] - ncp - qs // CMP_STRIDE, tq // CMP_STRIDE), ncp), :]

    qg, o_cmp = [], []
    for g in range(B_KV):
        qa = q_ref[:, 2 * g * LANE:(2 * g + 1) * LANE].astype(F32) * scale
        qb = q_ref[:, (2 * g + 1) * LANE:(2 * g + 2) * LANE].astype(F32) * scale
        q = jnp.concatenate([jnp.where(lo, qa, 0.0), jnp.where(lo, 0.0, qa),
                             jnp.where(lo, qb, 0.0), jnp.where(lo, 0.0, qb)], axis=0)
        q = q.T.astype(BF16)
        qg.append(q)

        st = _dot(kcmp_ref[0, :, g * LANE:(g + 1) * LANE], q) + cmask
        e = jnp.exp2(st - jnp.max(st, axis=0, keepdims=True))
        inv = jnp.where(tpos >= CMP_LEN - 1, 1.0 / jnp.sum(e, axis=0, keepdims=True), 0.0)
        p = e * inv
        o_cmp.append(_dot_tn(vcmp_ref[0, :, g * LANE:g * LANE + B_HD], p.astype(BF16)))

        psum = p[:, 0:tq] + p[:, tq:2 * tq] + p[:, 2 * tq:3 * tq] + p[:, 3 * tq:4 * tq]
        hi = psum.astype(BF16)
        lo_part = (psum - hi.astype(F32)).astype(BF16)
        imp = _dot(ovt, hi) + _dot(ovt, lo_part)
        imp = jnp.where(forced, BIG, jnp.where(future, NEG, imp))
        imp = jnp.where(j < nsel, imp, -jnp.inf)
        groups = [imp[gb * 8:(gb + 1) * 8, :] for gb in range(nselp // 8)]
        ranks = [jnp.zeros((8, tq), F32) for _ in groups]
        for ii in range(nsel):
            r = imp[ii:ii + 1, :]
            for gb, blk in enumerate(groups):
                if gb < ii // 8:
                    beats = r > blk
                elif gb > ii // 8:
                    beats = r >= blk
                else:
                    beats = (r > blk) | ((r == blk) & (j8 > ii % 8))
                ranks[gb] = jnp.where(beats, ranks[gb] + 1.0, ranks[gb])
        rank = jnp.concatenate(ranks, axis=0)
        bias_t = jnp.where((rank < float(min(SEL_N, nsel))) & (j < nsel), 0.0, NEG)
        bias_rows = jnp.concatenate([bias_t, jnp.zeros((LANE - nselp, tq), F32)], axis=0).astype(BF16)
        for r in range(B_GQ):
            qx_sc[g * B_GQ + r] = jnp.concatenate([q[:, r * tq:(r + 1) * tq], bias_rows], axis=0)

    m_sc[...] = jnp.full(m_sc.shape, NEG, F32)
    l_sc[...] = jnp.zeros(l_sc.shape, F32)
    acc_sc[...] = jnp.zeros(acc_sc.shape, F32)

    def step(k0, diagonal):
        fns, vs = [], []
        for g in range(B_KV):
            for r in range(B_GQ):
                def score(g=g, r=r):
                    kk = jnp.concatenate([ks_ref[pl.ds(k0, kc), g * LANE:(g + 1) * LANE],
                                          blk_ref[pl.ds(k0, kc), :]], axis=1)
                    st = _dot(kk, qx_sc[g * B_GQ + r])
                    return st + cb_ref[...] if diagonal else st
                fns.append(score)
                vs.append(vs_ref[pl.ds(k0, kc), g * LANE:g * LANE + B_HD])
        _online_steps(fns, vs, m_sc, l_sc, acc_sc)

    step(pl.multiple_of(qs, tq), True)

    wlen = min(WIN + tq, seq)
    w0 = pl.multiple_of(jnp.clip(qs - WIN, 0, seq - wlen), tq)
    wb = wb_ref[pl.ds(pl.multiple_of(WIN - (qs - w0), tq), wlen), :]
    g_all = jax.nn.sigmoid(glt_ref[...])
    gate = lambda g, r, n: g_all[3 * (g * B_GQ + r) + n:3 * (g * B_GQ + r) + n + 1, :]
    for g in range(B_KV):
        st = _dot(kw_ref[pl.ds(w0, wlen), g * LANE:(g + 1) * LANE], qg[g]) + wb
        e = jnp.exp2(st - jnp.max(st, axis=0, keepdims=True))
        o_win = _dot_tn(vw_ref[pl.ds(w0, wlen), g * LANE:g * LANE + B_HD], e.astype(BF16))
        o_win = o_win * (1.0 / jnp.sum(e, axis=0, keepdims=True))
        for r in range(B_GQ):
            cs = slice(r * tq, (r + 1) * tq)
            part_sc[g, :, cs] = gate(g, r, 0) * o_cmp[g][:, cs] + gate(g, r, 2) * o_win[:, cs]

    def body(c, carry):
        step(pl.multiple_of(2 * c * kc, kc), False)
        step(pl.multiple_of((2 * c + 1) * kc, kc), False)
        return carry

    lax.fori_loop(0, i // 2, body, 0)

    @pl.when(i % 2 == 1)
    def _():
        step(pl.multiple_of((i - 1) * kc, kc), False)

    for g in range(B_KV):
        outs = []
        for r in range(B_GQ):
            cs = slice(r * tq, (r + 1) * tq)
            o_sel = acc_sc[g * B_GQ + r] * (1.0 / l_sc[g * B_GQ + r])
            outs.append(part_sc[g, :, cs] + gate(g, r, 1) * o_sel)
        o_ref[:, 2 * g * LANE:(2 * g + 1) * LANE] = jnp.concatenate(outs[0:2], axis=0).T.astype(BF16)
        o_ref[:, (2 * g + 1) * LANE:(2 * g + 2) * LANE] = jnp.concatenate(outs[2:4], axis=0).T.astype(BF16)


def _nsa(p_main, cmp_k, cmp_v, gate_logits_t, ovt, block_onehot, causal_bias, window_bias, cmp_mask, bsz, seq, tq):
    n = p_main.shape[0]
    nq = seq // tq
    kv_spec = lambda col: pl.BlockSpec((seq, B_KV * LANE), lambda b, i, col=col: (b, col // B_KV))
    nrow = cmp_k.shape[1]
    cmp_spec = pl.BlockSpec((1, nrow, B_KV * LANE), lambda b, i: (b, 0, 0))
    nlane = B_GQ * tq
    return pl.pallas_call(
        functools.partial(_nsa_kernel, tq=tq, seq=seq),
        grid=(bsz, nq),
        in_specs=[
            pl.BlockSpec((tq, BRANCH_W), lambda b, i: (b * nq + i, COL_B_Q // 4)),
            kv_spec(COL_KS), kv_spec(COL_VS), kv_spec(COL_KW), kv_spec(COL_VW),
            cmp_spec, cmp_spec,
            pl.BlockSpec((3 * B_HEADS, tq), lambda b, i: (SMALL_G0 // (3 * B_HEADS), b * nq + i)),
            _const_spec(ovt.shape), _const_spec(block_onehot.shape), _const_spec(causal_bias.shape),
            _const_spec(window_bias.shape), _const_spec(cmp_mask.shape),
        ],
        out_specs=pl.BlockSpec((tq, BRANCH_W), lambda b, i: (b * nq + i, 0)),
        out_shape=jax.ShapeDtypeStruct((n, BRANCH_W), BF16),
        scratch_shapes=[pltpu.VMEM((B_HEADS, 2 * LANE, tq), BF16), pltpu.VMEM((B_HEADS, 1, tq), F32),
                        pltpu.VMEM((B_HEADS, 1, tq), F32), pltpu.VMEM((B_HEADS, B_HD, tq), F32),
                        pltpu.VMEM((B_KV, B_HD, nlane), F32)],
        compiler_params=_params(2),
        name="nsa",
    )(p_main, p_main, p_main, p_main, p_main, cmp_k, cmp_v, gate_logits_t, ovt, block_onehot, causal_bias, window_bias,
      cmp_mask)


def _fcum_kernel(f_ref, b_ref, u_ref, m_ref, o_ref, kaug_ref):
    ntile = f_ref.shape[1] // LANE
    x = jnp.concatenate([f_ref[:, t * LANE:(t + 1) * LANE] for t in range(ntile)], axis=0) + b_ref[...]
    lf = jnp.minimum(x, 0.0) - jnp.log(1.0 + jnp.exp(-jnp.abs(x)))
    u = u_ref[...]
    within = sum(_dot(t, u) for t in _split3(lf))
    mm = m_ref[...]
    before = sum(_dot(mm, t) for t in _split3(within))
    c = (within + before[:, LANE - 1:LANE]) * LOG2E
    o_ref[0] = c
    hi, mid, lo = (t.astype(F32) for t in _split3(c))
    ones = jnp.ones((C_HEADS, LANE), F32)
    pad = jnp.zeros((LANE - 4 * C_HEADS, LANE), F32)
    for t in range(c.shape[0] // C_HEADS):
        rs = slice(t * C_HEADS, (t + 1) * C_HEADS)
        stacked = jnp.concatenate([hi[rs], mid[rs], lo[rs], ones, pad], axis=0)
        kaug_ref[0, t * LANE:(t + 1) * LANE, :] = stacked.T.astype(BF16)


def _fcum(p_small_t, bias_col, umat, mmat, bsz, seq):
    nrow = seq // LANE * C_HEADS
    spec = pl.BlockSpec((1, nrow, LANE), lambda b: (b, 0, 0))
    return pl.pallas_call(
        _fcum_kernel,
        grid=(bsz,),
        in_specs=[pl.BlockSpec((C_HEADS, seq), lambda b: (SMALL_F0 // C_HEADS, b)),
                  _const_spec((nrow, 1)), _const_spec((LANE, LANE)), _const_spec((nrow, nrow))],
        out_specs=[spec, pl.BlockSpec((1, seq, LANE), lambda b: (b, 0, 0))],
        out_shape=[jax.ShapeDtypeStruct((bsz, nrow, LANE), F32), jax.ShapeDtypeStruct((bsz, seq, LANE), BF16)],
        compiler_params=_params(1),
        name="fcum",
    )(p_small_t, bias_col, umat, mmat)


def _fox_kernel(q_ref, k_ref, v_ref, kaug_ref, crow_ref, cb_ref, o_ref, m_sc, l_sc, acc_sc, *, tq):
    i = pl.program_id(1)
    kc = tq
    scale = C_HD ** -0.5 * LOG2E
    row = lax.broadcasted_iota(jnp.int32, (LANE, tq), 0)
    top = row < C_HD
    crow = crow_ref[0]
    qxs = []
    for pr in range(FOX_PAIRS):
        q2t = (q_ref[:, pr * LANE:(pr + 1) * LANE].astype(F32) * scale).T
        halves = []
        for hd in range(2):
            h = 2 * pr + hd
            ct = jnp.concatenate([crow[t * C_HEADS + h:t * C_HEADS + h + 1, :] for t in range(tq // LANE)], axis=1)
            hi, mid, lw = (t.astype(F32) for t in _split3(ct))
            aug = jnp.where((row == h) | (row == C_HEADS + h) | (row == 2 * C_HEADS + h), -1.0, jnp.where(
                row == 3 * C_HEADS, hi, jnp.where(row == 3 * C_HEADS + 1, mid, jnp.where(
                    row == 3 * C_HEADS + 2, lw, 0.0))))
            qh = jnp.where(top, q2t, 0.0) if hd == 0 else jnp.where(top, 0.0, q2t)
            halves.append(jnp.concatenate([qh, aug], axis=0))
        qxs.append(jnp.concatenate(halves, axis=1).astype(BF16))

    m_sc[...] = jnp.full(m_sc.shape, NEG, F32)
    l_sc[...] = jnp.zeros(l_sc.shape, F32)
    acc_sc[...] = jnp.zeros(acc_sc.shape, F32)

    def step(c, diagonal):
        k0 = pl.multiple_of(c * kc, kc)
        fns = []
        for pr in range(FOX_PAIRS):
            cols = slice(pr * LANE, (pr + 1) * LANE)
            for hd in range(2):
                def score(pr=pr, hd=hd, cols=cols):
                    kx = jnp.concatenate([k_ref[pl.ds(k0, kc), cols], kaug_ref[0, pl.ds(k0, kc), :]], axis=1)
                    st = _dot(kx, qxs[pr][:, hd * tq:(hd + 1) * tq])
                    return st + cb_ref[...] if diagonal else st
                fns.append(score)
        vs = [v_ref[pl.ds(k0, kc), hh * C_HD:(hh + 1) * C_HD] for hh in range(2 * FOX_PAIRS)]
        _online_steps(fns, vs, m_sc, l_sc, acc_sc)

    def body(c, carry):
        step(2 * c, False)
        step(2 * c + 1, False)
        return carry

    lax.fori_loop(0, i // 2, body, 0)

    @pl.when(i % 2 == 1)
    def _():
        step(i - 1, False)
        step(i, True)

    @pl.when(i % 2 == 0)
    def _():
        step(i, True)

    for pr in range(FOX_PAIRS):
        ot = jnp.concatenate([acc_sc[2 * pr + hd] * (1.0 / l_sc[2 * pr + hd]) for hd in range(2)], axis=0)
        o_ref[:, pr * LANE:(pr + 1) * LANE] = ot.T.astype(BF16)


def _fox(p_main, k_aug, c_rows, causal_bias, bsz, seq, tq):
    assert 2 * FOX_PAIRS == C_HEADS
    n = p_main.shape[0]
    nq = seq // tq
    return pl.pallas_call(
        functools.partial(_fox_kernel, tq=tq),
        grid=(bsz, nq),
        in_specs=[
            pl.BlockSpec((tq, BRANCH_W), lambda b, i: (b * nq + i, COL_C_Q // 4)),
            pl.BlockSpec((seq, BRANCH_W), lambda b, i: (b, COL_C_K // 4)),
            pl.BlockSpec((seq, BRANCH_W), lambda b, i: (b, COL_C_V // 4)),
            pl.BlockSpec((1, seq, LANE), lambda b, i: (b, 0, 0)),
            pl.BlockSpec((1, tq // LANE * C_HEADS, LANE), lambda b, i: (b, i, 0)),
            _const_spec(causal_bias.shape),
        ],
        out_specs=pl.BlockSpec((tq, BRANCH_W), lambda b, i: (b * nq + i, 0)),
        out_shape=jax.ShapeDtypeStruct((n, BRANCH_W), BF16),
        scratch_shapes=[pltpu.VMEM((C_HEADS, 1, tq), F32), pltpu.VMEM((C_HEADS, 1, tq), F32),
                        pltpu.VMEM((C_HEADS, C_HD, tq), F32)],
        compiler_params=_params(2),
        name="fox",
    )(p_main, p_main, p_main, k_aug, c_rows, causal_bias)


def _memkv_kernel(mem_ref, g_ref, w_ref, o_ref):
    h = _rms(mem_ref[0], g_ref[...]).astype(BF16)
    o_ref[0] = _dot(h, w_ref[...]).astype(BF16)


def _memkv(mem, g_mem, w_kv):
    bsz, mlen, _ = mem.shape
    return pl.pallas_call(
        _memkv_kernel,
        grid=(bsz,),
        in_specs=[
            pl.BlockSpec((1, mlen, D_MODEL), lambda b: (b, 0, 0)),
            _const_spec((1, D_MODEL)), _const_spec((D_MODEL, 2 * BRANCH_W)),
        ],
        out_specs=pl.BlockSpec((1, mlen, 2 * BRANCH_W), lambda b: (b, 0, 0)),
        out_shape=jax.ShapeDtypeStruct((bsz, mlen, 2 * BRANCH_W), BF16),
        compiler_params=_params(1),
        name="memkv",
    )(mem, g_mem, w_kv)


def _merge_kernel(x_ref, gpre_ref, gpost_ref, ya_ref, yb_ref, yc_ref, bz_ref, cz_ref, mq_ref, mz_ref, mkv_ref,
                  wg0_ref, wg1_ref, wg2_ref, wg3_ref, wbr_ref, wo_ref, o_ref):
    x = x_ref[...]
    h = _rms(x, gpre_ref[...]).astype(BF16)

    wg_refs = (wg0_ref, wg1_ref, wg2_ref, wg3_ref)

    def branch(nb, y):
        gate = jax.nn.sigmoid(_dot(h, wg_refs[nb][...]))
        return gate * _dot(y, wbr_ref[nb])

    merged = branch(0, ya_ref[...])
    mq = mq_ref[...]
    scale = M_HD ** -0.5
    ss = [_dot_nt(mq[:, hd * M_HD:(hd + 1) * M_HD], mkv_ref[0, :, hd * M_HD:(hd + 1) * M_HD]) * scale
          for hd in range(M_HEADS)]
    merged = merged + branch(1, (yb_ref[...].astype(F32) * _silu(bz_ref[...].astype(F32))).astype(BF16))
    ym = []
    for hd in range(M_HEADS):
        e = jnp.exp(ss[hd] - jnp.max(ss[hd], axis=-1, keepdims=True))
        pv = _dot(e.astype(BF16), mkv_ref[0, :, BRANCH_W + hd * M_HD:BRANCH_W + (hd + 1) * M_HD])
        ym.append(pv / jnp.sum(e, axis=-1, keepdims=True))
    merged = merged + branch(2, (yc_ref[...].astype(F32) * _silu(cz_ref[...].astype(F32))).astype(BF16))
    y_m = jnp.concatenate(ym, axis=-1)
    merged = merged + branch(3, (y_m * _silu(mz_ref[...].astype(F32))).astype(BF16))
    out = _dot(merged.astype(BF16), wo_ref[...])
    o_ref[...] = x + _rms(out, gpost_ref[...])


def _merge(x2, g_pre, g_post, y_a, y_b, y_c, p_main, mkv, w_gate, w_br, w_o, seq, tz):
    n = x2.shape[0]
    nq = seq // tz
    mlen = mkv.shape[1]
    act = lambda c: pl.BlockSpec((tz, BRANCH_W), lambda i, c=c: (i, c))
    return pl.pallas_call(
        _merge_kernel,
        grid=(n // tz,),
        in_specs=[
            pl.BlockSpec((tz, D_MODEL), lambda i: (i, 0)),
            _const_spec((1, D_MODEL)), _const_spec((1, D_MODEL)),
            act(0), act(0), act(0),
            act(COL_B_Z // 4), act(COL_C_Z // 4), act(COL_M_Q // 4), act(COL_M_Z // 4),
            pl.BlockSpec((1, mlen, 2 * BRANCH_W), lambda i: (i // nq, 0, 0)),
            *[_const_spec((D_MODEL, D_MODEL))] * N_BRANCH,
            _const_spec((N_BRANCH, BRANCH_W, D_MODEL)),
            _const_spec((D_MODEL, D_MODEL)),
        ],
        out_specs=pl.BlockSpec((tz, D_MODEL), lambda i: (i, 0)),
        out_shape=jax.ShapeDtypeStruct((n, D_MODEL), F32),
        compiler_params=_params(1),
        name="merge",
    )(x2, g_pre, g_post, y_a, y_b, y_c, p_main, p_main, p_main, p_main, mkv, *w_gate, w_br, w_o)


def _pack_w_in(w):
    (a_u, a_v, a_z, b_q, b_kc, b_vc, b_ks, b_vs, b_kw, b_vw, b_g, b_z,
     c_q, c_k, c_v, c_f, c_z, m_q, m_z) = jnp.split(w, IN_SPLITS, axis=1)
    dup = lambda t: jnp.concatenate([t[:, :B_HD], t[:, :B_HD], t[:, B_HD:], t[:, B_HD:]], axis=1)
    main = jnp.concatenate([a_u, a_v, a_z, b_q, b_z, c_q, c_k, c_v, c_z, m_q, m_z,
                            dup(b_ks), dup(b_kw), dup(b_vs), dup(b_vw)], axis=1)
    small = jnp.concatenate([b_g, c_f, jnp.zeros((w.shape[0], LANE - b_g.shape[1] - c_f.shape[1]), w.dtype)], axis=1)
    return main.astype(BF16), small.T.astype(BF16), jnp.concatenate([b_kc, b_vc], axis=1).astype(BF16)


def _pack_compress(pos, w1, w2):
    eye = jnp.eye(B_KV, dtype=F32)
    w1r = w1.reshape(2, CMP_STRIDE, B_HD, B_HD)
    w1big = jnp.einsum('hlde,gk->hlgdke', w1r, eye).reshape(2, CMP_STRIDE * B_KV * B_HD, B_KV * B_HD)
    w2big = jnp.einsum('de,gk,u->gdkue', w2, eye, jnp.ones((2,), F32)).reshape(B_KV * B_HD, 2 * B_KV * B_HD)
    pos2 = jnp.broadcast_to(pos.reshape(2, CMP_STRIDE, 1, B_HD), (2, CMP_STRIDE, B_KV, B_HD)).reshape(2, -1)
    return pos2.astype(F32), w1big.astype(BF16), w2big.astype(BF16)


def _rope_tables(seq):
    half = B_HD // 2
    freqs = ROPE_THETA ** (-jnp.arange(half, dtype=F32) / half)
    ang = jnp.arange(seq, dtype=F32)[:, None] * freqs[None, :]
    cos, sin = jnp.cos(ang), jnp.sin(ang)
    cos_t = jnp.concatenate([cos, cos, cos, cos], axis=1)
    sin_t = jnp.concatenate([-sin, sin, -sin, sin], axis=1)
    return dict(cos=cos_t, sin=sin_t)


def _causal_bias(tq, nlane):
    k = np.arange(tq)[:, None]
    t = (np.arange(nlane) & (tq - 1))[None, :]
    return jnp.asarray(np.where(k <= t, 0.0, NEG), F32)


def _nsa_constants(seq, tq):
    nc = (seq - CMP_LEN) // CMP_STRIDE + 1
    nsel = seq // SEL_LEN
    nselp = -(-nsel // 8) * 8
    ncp = seq // CMP_STRIDE
    ci = np.arange(ncp) * CMP_STRIDE
    sj = np.arange(nselp) * SEL_LEN
    ovt = ((ci[None, :] <= sj[:, None] + SEL_LEN - 1) & (ci[None, :] + CMP_LEN - 1 >= sj[:, None])
           & (np.arange(ncp)[None, :] < nc) & (np.arange(nselp)[:, None] < nsel))
    nlane = B_GQ * tq
    r = np.arange(2 * WIN + tq)[:, None]
    t = (np.arange(nlane) & (tq - 1))[None, :]
    wb = np.where((r > t) & (r <= t + WIN), 0.0, NEG)
    onehot = (np.arange(seq)[:, None] // SEL_LEN) == np.arange(LANE)[None, :]
    rows = 2 * ncp - tq // CMP_STRIDE
    rr = (np.arange(rows) - (rows - ncp))[:, None]
    cm = np.where(CMP_STRIDE * rr + CMP_LEN - 1 <= t, 0.0, NEG)
    return dict(ovt=jnp.asarray(ovt, BF16), nsa_blk=jnp.asarray(onehot, BF16), nsa_cb=_causal_bias(tq, tq),
                nsa_wb=jnp.asarray(wb, F32), nsa_cm=jnp.asarray(cm, F32))


def _fcum_constants(seq):
    ntile = seq // LANE
    nrow = C_HEADS * ntile
    umat = np.arange(LANE)[:, None] <= np.arange(LANE)[None, :]
    r = np.arange(nrow)
    mmat = (r[:, None] % C_HEADS == r[None, :] % C_HEADS) & (r[None, :] // C_HEADS < r[:, None] // C_HEADS)
    return dict(umat=jnp.asarray(umat, BF16), mmat=jnp.asarray(mmat, BF16))


def _layer(x2, mem, bsz, seq, tb, w_in, g_pre, g_post, g_mem, w_mem_kv, a_ln_g, a_ln_b, a_ws, a_bs,
           pos_k, w1_k, w2_k, pos_v, w1_v, w2_v, c_fbias, w_br, w_gate, w_o):
    w_main, w_small, w_cmp = _pack_w_in(w_in)
    p_main, p_small_t, kc, vc, y_a = _proj(x2, g_pre.reshape(1, -1), w_main, w_small, w_cmp, tb['cos'], tb['sin'],
                                           a_ln_g.reshape(1, -1), a_ln_b.reshape(1, -1), a_ws, a_bs.T, seq,
                                           tm=min(512, seq))

    cmp_k, cmp_v = _compress(kc, vc, _pack_compress(pos_k, w1_k, w2_k), _pack_compress(pos_v, w1_v, w2_v), bsz)
    y_b = _nsa(p_main, cmp_k, cmp_v, p_small_t, tb['ovt'], tb['nsa_blk'], tb['nsa_cb'], tb['nsa_wb'], tb['nsa_cm'],
               bsz, seq, tq=NSA_TQ)

    ntile = seq // LANE
    bias_col = jnp.tile(c_fbias.astype(F32), ntile).reshape(-1, 1)
    c_rows, k_aug = _fcum(p_small_t, bias_col, tb['umat'], tb['mmat'], bsz, seq)
    y_c = _fox(p_main, k_aug, c_rows, tb['fox_cb'], bsz, seq, tq=FOX_TQ)

    mkv = _memkv(mem, g_mem.reshape(1, -1), w_mem_kv.astype(BF16))
    wg = [w_gate[:, nb, :].astype(BF16) for nb in range(N_BRANCH)]
    return _merge(x2, g_pre.reshape(1, -1), g_post.reshape(1, -1), y_a, y_b, y_c, p_main, mkv,
                  wg, w_br.astype(BF16), w_o.astype(BF16), seq, tz=min(512, seq))


def kernel(x, mem, w_in, g_pre, g_post, g_mem, w_mem_kv, a_ln_g, a_ln_b, a_ws, a_bs, b_cmp_pos_k, b_cmp_w1_k,
           b_cmp_w2_k, b_cmp_pos_v, b_cmp_w1_v, b_cmp_w2_v, c_fbias, w_br, w_gate, w_o):
    bsz, seq, d = x.shape
    tables = dict(**_rope_tables(seq), **_nsa_constants(seq, NSA_TQ), **_fcum_constants(seq),
                  fox_cb=_causal_bias(FOX_TQ, FOX_TQ))
    x2 = x.reshape(bsz * seq, d)
    for l in range(w_in.shape[0]):
        x2 = _layer(x2, mem, bsz, seq, tables, w_in[l], g_pre[l], g_post[l], g_mem[l], w_mem_kv[l],
                    a_ln_g[l], a_ln_b[l], a_ws[l], a_bs[l], b_cmp_pos_k[l], b_cmp_w1_k[l], b_cmp_w2_k[l],
                    b_cmp_pos_v[l], b_cmp_w1_v[l], b_cmp_w2_v[l], c_fbias[l], w_br[l], w_gate[l], w_o[l])
    return x2.reshape(bsz, seq, d)
```
